```python
import math, functools
import jax, jax.numpy as jnp
from jax import lax
import numpy as np

D_MODEL = 1024
BATCH = 16
SEQ = 4096
DEPTH = 1
DEC_BATCH = 8
DEC_SEQ = 16
PAST_LEN = 1024

CHUNK = 64
MIX_WIDTH = D_MODEL
M_HEADS = 4
R_HEADS = 4
M_WIDTH = MIX_WIDTH // 2
R_WIDTH = MIX_WIDTH - M_WIDTH
M_DH = M_WIDTH // M_HEADS
R_DH = R_WIDTH // R_HEADS
IN_WIDTH = 4 * M_WIDTH + 2 * M_HEADS + 4 * R_WIDTH
I_OFF = 4 * M_WIDTH
F_OFF = I_OFF + M_HEADS
ROPE_BASE = 10000.0
PEER_HEADS = 8
PEER_NKEYS = 128
PEER_NEXP = PEER_NKEYS * PEER_NKEYS
PEER_DQ = 256
PEER_TOPK = 16
PEER_BLOCK = 128
LN_EPS = 1e-5
HEAD_EPS = 1e-6
ALPHA = (2.0 * DEPTH) ** 0.25
BETA = (8.0 * DEPTH) ** -0.25

kernel_name = "hymba_mlstm_retention_peer_stream_step"

F32 = jnp.float32


def layer_norm(x, g=None, b=None):
    xf = x.astype(F32)
    mu = jnp.mean(xf, axis=-1, keepdims=True)
    var = jnp.mean(jnp.square(xf - mu), axis=-1, keepdims=True)
    y = (xf - mu) * lax.rsqrt(var + LN_EPS)
    if g is not None:
        y = y * g.astype(F32) + b.astype(F32)
    return y.astype(x.dtype)


def head_norm(t):
    mu = jnp.mean(t, axis=-1, keepdims=True)
    var = jnp.mean(jnp.square(t - mu), axis=-1, keepdims=True)
    return (t - mu) * lax.rsqrt(var + HEAD_EPS)


def rotary(t, pos0):
    T, dh = t.shape[-2], t.shape[-1]
    half = dh // 2
    inv = ROPE_BASE ** (-jnp.arange(half, dtype=F32) / half)
    ang = (pos0 + jnp.arange(T, dtype=F32))[:, None] * inv[None, :]
    cos, sin = jnp.cos(ang), jnp.sin(ang)
    t1, t2 = t[..., :half], t[..., half:]
    return jnp.concatenate([t1 * cos - t2 * sin, t1 * sin + t2 * cos], axis=-1)


def mlstm_chunk(carry, inp):
    C, n, m = carry
    q, k, v, ig, lf = inp
    L = q.shape[2]
    b = jnp.cumsum(lf, axis=-1)
    a = ig - b
    M = jnp.maximum(m[..., None], lax.cummax(a, axis=2))
    causal = jnp.tril(jnp.ones((L, L), dtype=bool))
    dmat = jnp.exp(jnp.where(causal, a[..., None, :] - M[..., :, None], -jnp.inf))
    inter = jnp.exp(m[..., None] - M)
    s = jnp.einsum("bhtd,bhsd->bhts", q, k) * dmat
    num = inter[..., None] * jnp.einsum("bhtd,bhde->bhte", q, C) + jnp.einsum("bhts,bhse->bhte", s, v)
    den = inter * jnp.einsum("bhtd,bhd->bht", q, n) + jnp.sum(s, axis=-1)
    h = num / jnp.maximum(jnp.abs(den), jnp.exp(-(b + M)))[..., None]
    M_end = M[..., -1]
    w_end = jnp.exp(a - M_end[..., None])
    decay = jnp.exp(m - M_end)
    kw = k * w_end[..., None]
    C_new = decay[..., None, None] * C + jnp.einsum("bhsd,bhse->bhde", kw, v)
    n_new = decay[..., None] * n + jnp.sum(kw, axis=2)
    m_new = b[..., -1] + M_end
    return (C_new, n_new, m_new), h


def retention_chunk(S, inp, log_gamma):
    q, k, v = inp
    L = q.shape[2]
    pos = jnp.arange(L, dtype=F32)
    diff = pos[:, None] - pos[None, :]
    dmat = jnp.where(diff >= 0, jnp.exp(log_gamma[:, None, None] * jnp.maximum(diff, 0.0)), 0.0)
    s = jnp.einsum("bhtd,bhsd->bhts", q, k) * dmat
    inter = jnp.exp(log_gamma[:, None] * (pos + 1.0))
    o = jnp.einsum("bhts,bhse->bhte", s, v) + inter[..., None] * jnp.einsum("bhtd,bhde->bhte", q, S)
    w_end = jnp.exp(log_gamma[:, None] * (L - 1.0 - pos))
    S_new = jnp.exp(log_gamma * L)[:, None, None] * S + jnp.einsum("bhsd,bhse->bhde", k * w_end[..., None], v)
    return S_new, o


def split_in(z):
    sizes = [M_WIDTH] * 4 + [M_HEADS] * 2 + [R_WIDTH] * 4
    idx = [int(i) for i in np.cumsum(sizes)[:-1]]
    return jnp.split(z, idx, axis=-1)


def token_mixers(h, pos0, C0, n0, m0, S0, w_in, b_in, head_g, w_out):
    B, T, _ = h.shape
    L = CHUNK if T % CHUNK == 0 else T
    nC = T // L
    z = h @ w_in + b_in
    mq, mk, mv, mo, mi, mf, rq, rk, rv, rg = split_in(z)

    def heads(t, H):
        t = t.reshape(B, T, H, -1).transpose(0, 2, 1, 3).astype(F32)
        return t

    def to_chunks(t):
        return t.reshape(B, t.shape[1], nC, L, t.shape[-1]).transpose(2, 0, 1, 3, 4)

    def gate_chunks(g):
        return g.astype(F32).transpose(0, 2, 1).reshape(B, -1, nC, L).transpose(2, 0, 1, 3)

    def from_chunks(t):
        return t.transpose(1, 0, 3, 2, 4).reshape(B, T, t.shape[2], t.shape[-1])

    q_a = to_chunks(heads(mq, M_HEADS))
    k_a = to_chunks(heads(mk, M_HEADS) * (M_DH ** -0.5))
    v_a = to_chunks(heads(mv, M_HEADS))
    ig = gate_chunks(mi)
    lf = gate_chunks(jax.nn.log_sigmoid(mf.astype(F32)))
    (C1, n1, m1), h_a = lax.scan(mlstm_chunk, (C0.astype(F32), n0.astype(F32), m0.astype(F32)),
                                 (q_a, k_a, v_a, ig, lf))
    out_a = head_norm(from_chunks(h_a)).reshape(B, T, M_WIDTH) * jax.nn.sigmoid(mo.astype(F32))

    log_gamma = jnp.log1p(-(2.0 ** (-5.0 - jnp.arange(R_HEADS, dtype=F32))))
    q_b = to_chunks(rotary(heads(rq, R_HEADS), pos0))
    k_b = to_chunks(rotary(heads(rk, R_HEADS), pos0) * (R_DH ** -0.5))
    v_b = to_chunks(heads(rv, R_HEADS))
    S1, h_b = lax.scan(functools.partial(retention_chunk, log_gamma=log_gamma), S0.astype(F32),
                       (q_b, k_b, v_b))
    out_b = head_norm(from_chunks(h_b)).reshape(B, T, R_WIDTH) * jax.nn.silu(rg.astype(F32))

    y = jnp.concatenate([out_a, out_b], axis=-1) * head_g.astype(F32)
    return y.astype(h.dtype) @ w_out, (C1, n1, m1, S1)


def peer(h, w_pq, sub_keys, expert_u, expert_v):
    B, T, D = h.shape
    x = h.reshape(B * T, D)
    ntok = x.shape[0]
    pad = (-ntok) % PEER_BLOCK
    xb = jnp.pad(x, ((0, pad), (0, 0))).reshape(-1, PEER_BLOCK, D)

    def block(xt):
        P = xt.shape[0]
        q = (xt @ w_pq).reshape(P, PEER_HEADS, 2, PEER_DQ // 2)
        s = jnp.einsum("phcd,cnd->phcn", q, sub_keys).astype(F32)
        sv, si = lax.top_k(s, PEER_TOPK)
        cand = (sv[..., 0, :, None] + sv[..., 1, None, :]).reshape(P, PEER_HEADS, -1)
        cidx = (si[..., 0, :, None] * PEER_NKEYS + si[..., 1, None, :]).reshape(P, PEER_HEADS, -1)
        tv, ti = lax.top_k(cand, PEER_TOPK)
        eidx = jnp.take_along_axis(cidx, ti, axis=-1)
        g = jax.nn.softmax(tv, axis=-1)
        u = jnp.take(expert_u, eidx, axis=0)
        act = jax.nn.gelu(jnp.einsum("phkd,pd->phk", u, xt).astype(F32), approximate=False)
        vv = jnp.take(expert_v, eidx, axis=0)
        return jnp.einsum("phk,phkd->pd", (g * act).astype(xt.dtype), vv)

    out = lax.map(block, xb).reshape(-1, D)[:ntok]
    return out.reshape(B, T, D)


def trunk_layer(x, c, pos0, C0, n0, m0, S0, w_ada, b_ada, w_in, b_in, head_g, w_out,
                ln1_g, ln1_b, w_pq, sub_keys, expert_u, expert_v, ln2_g, ln2_b):
    mod = jax.nn.silu(c) @ w_ada + b_ada
    sh1, sc1, g1, sh2, sc2, g2 = jnp.split(mod[:, None, :], 6, axis=-1)
    h = layer_norm(x) * (1.0 + sc1) + sh1
    mix, st = token_mixers(h, pos0, C0, n0, m0, S0, w_in, b_in, head_g, w_out)
    x = layer_norm(ALPHA * x + g1 * mix, ln1_g, ln1_b)
    h = layer_norm(x) * (1.0 + sc2) + sh2
    x = layer_norm(ALPHA * x + g2 * peer(h, w_pq, sub_keys, expert_u, expert_v), ln2_g, ln2_b)
    return x, st


def setup_inputs(seed: int = 0) -> dict:
    key = jax.random.key(seed)
    ks = jax.random.split(key, 24)
    nrm = lambda k, shp, s: jax.random.normal(k, shp, F32) * s
    D = D_MODEL
    b_in = nrm(ks[10], (DEPTH, IN_WIDTH), 0.02)
    b_in = b_in.at[:, F_OFF:F_OFF + M_HEADS].add(jnp.linspace(3.0, 6.0, M_HEADS))
    w_in = nrm(ks[9], (DEPTH, D, IN_WIDTH), D ** -0.5)
    vcols = jnp.zeros((IN_WIDTH,), F32).at[2 * M_WIDTH:3 * M_WIDTH].set(1.0)
    vcols = vcols.at[F_OFF + M_HEADS + 2 * R_WIDTH:F_OFF + M_HEADS + 3 * R_WIDTH].set(1.0)
    w_in = w_in * (1.0 + (BETA - 1.0) * vcols)
    return {
        "x_prompt": nrm(ks[0], (BATCH, SEQ, D), 1.0),
        "x_sample": nrm(ks[1], (DEC_BATCH, DEC_SEQ, D), 1.0),
        "c_prompt": nrm(ks[2], (BATCH, D), 1.0),
        "c_sample": nrm(ks[3], (DEC_BATCH, D), 1.0),
        "state_mlstm_C": nrm(ks[4], (DEPTH, DEC_BATCH, M_HEADS, M_DH, M_DH), 0.1),
        "state_mlstm_n": nrm(ks[5], (DEPTH, DEC_BATCH, M_HEADS, M_DH), 0.5),
        "state_mlstm_m": nrm(ks[6], (DEPTH, DEC_BATCH, M_HEADS), 0.5),
        "state_ret_S": nrm(ks[7], (DEPTH, DEC_BATCH, R_HEADS, R_DH, R_DH), 0.1),
        "w_ada": nrm(ks[8], (DEPTH, D, 6 * D), D ** -0.5),
        "b_ada": nrm(ks[11], (DEPTH, 6 * D), 0.02),
        "w_in": w_in,
        "b_in": b_in,
        "head_g": 1.0 + nrm(ks[12], (DEPTH, MIX_WIDTH), 0.02),
        "w_out": nrm(ks[13], (DEPTH, MIX_WIDTH, D), BETA * MIX_WIDTH ** -0.5),
        "ln1_g": 1.0 + nrm(ks[14], (DEPTH, D), 0.02),
        "ln1_b": nrm(ks[15], (DEPTH, D), 0.02),
        "w_pq": nrm(ks[16], (DEPTH, D, PEER_HEADS * PEER_DQ), D ** -0.5),
        "sub_keys": nrm(ks[17], (DEPTH, 2, PEER_NKEYS, PEER_DQ // 2), (PEER_DQ // 2) ** -0.5),
        "expert_u": nrm(ks[18], (DEPTH, PEER_NEXP, D), D ** -0.5),
        "expert_v": nrm(ks[19], (DEPTH, PEER_NEXP, D), BETA * PEER_HEADS ** -0.5),
        "ln2_g": 1.0 + nrm(ks[20], (DEPTH, D), 0.02),
        "ln2_b": nrm(ks[21], (DEPTH, D), 0.02),
    }


def reference(x_prompt, x_sample, c_prompt, c_sample, state_mlstm_C, state_mlstm_n, state_mlstm_m,
              state_ret_S, w_ada, b_ada, w_in, b_in, head_g, w_out, ln1_g, ln1_b, w_pq, sub_keys,
              expert_u, expert_v, ln2_g, ln2_b):
    def run(x, c, pos0, C0, n0, m0, S0):
        Cs, ns, ms, Ss = [], [], [], []
        for l in range(DEPTH):
            x, (C1, n1, m1, S1) = trunk_layer(
                x, c, pos0, C0[l], n0[l], m0[l], S0[l], w_ada[l], b_ada[l], w_in[l], b_in[l],
                head_g[l], w_out[l], ln1_g[l], ln1_b[l], w_pq[l], sub_keys[l], expert_u[l],
                expert_v[l], ln2_g[l], ln2_b[l])
            Cs.append(C1); ns.append(n1); ms.append(m1); Ss.append(S1)
        return x, jnp.stack(Cs), jnp.stack(ns), jnp.stack(ms), jnp.stack(Ss)

    B = x_prompt.shape[0]
    zC = jnp.zeros((DEPTH, B, M_HEADS, M_DH, M_DH), F32)
    zn = jnp.zeros((DEPTH, B, M_HEADS, M_DH), F32)
    zm = jnp.zeros((DEPTH, B, M_HEADS), F32)
    zS = jnp.zeros((DEPTH, B, R_HEADS, R_DH, R_DH), F32)
    y_prompt, C_p, n_p, m_p, S_p = run(x_prompt, c_prompt, 0, zC, zn, zm, zS)
    y_sample, C_s, n_s, m_s, S_s = run(x_sample, c_sample, PAST_LEN, state_mlstm_C, state_mlstm_n,
                                       state_mlstm_m, state_ret_S)
    return (y_prompt, y_sample, C_p, n_p, m_p, S_p, C_s, n_s, m_s, S_s)
```

```python
import functools
import math

import jax
import jax.numpy as jnp
from jax import lax
from jax.experimental import pallas as pl
from jax.experimental.pallas import tpu as pltpu

F32 = jnp.float32
BF16 = jnp.bfloat16
HIGHEST = lax.Precision.HIGHEST

LN_EPS = 1e-5
HEAD_EPS = 1e-6
ROPE_BASE = 10000.0
N_HEADS = 4
DH = 128
LANES = 128
SUBLANES = 8
PEER_HEADS = 8
PEER_TOPK = 16
MIX_CHUNK = 256
POST_TILE = 128
PEER_TILE = 64
ROW_WORDS = 512
ROW_SUB = ROW_WORDS // LANES
SLOT_STRIDE = 129
MASKED_GATE = -1e30
VMEM_LIMIT = 56 * 1024 * 1024


def _ln(x, eps):
    mu = jnp.mean(x, axis=-1, keepdims=True)
    xc = x - mu
    var = jnp.mean(xc * xc, axis=-1, keepdims=True)
    return xc * lax.rsqrt(var + eps)


def _dot(a, b):
    return jnp.dot(a.astype(BF16), b.astype(BF16), preferred_element_type=F32)


def _dot_nt(a, b):
    return lax.dot_general(a.astype(BF16), b.astype(BF16), (((1,), (1,)), ((), ())),
                           preferred_element_type=F32)


def _dot_tn(a, b):
    return lax.dot_general(a.astype(BF16), b.astype(BF16), (((0,), (0,)), ((), ())),
                           preferred_element_type=F32)


def _log_sigmoid(x):
    return jnp.minimum(x, 0.0) - jnp.log1p(jnp.exp(-jnp.abs(x)))


def _ada_kernel(c_ref, w_ref, b_ref, o_ref):
    c = c_ref[...]
    s = c * jax.nn.sigmoid(c)
    o_ref[...] = jnp.dot(s, w_ref[...], preferred_element_type=F32, precision=HIGHEST) + b_ref[...]


def _ada(c, w_ada, b_ada):
    B, D = c.shape
    n = w_ada.shape[1]
    return pl.pallas_call(
        _ada_kernel,
        grid=(n // D,),
        in_specs=[pl.BlockSpec((B, D), lambda j: (0, 0)),
                  pl.BlockSpec((D, D), lambda j: (0, j)),
                  pl.BlockSpec((1, D), lambda j: (0, j))],
        out_specs=pl.BlockSpec((B, D), lambda j: (0, j)),
        out_shape=jax.ShapeDtypeStruct((B, n), F32),
        name="ada",
    )(c, w_ada, b_ada.reshape(1, n))


def _inproj_kernel(x_ref, sh_ref, sc_ref, wm_ref, bm_ref, wr_ref, br_ref, wg_ref, bg_ref,
                   wgt_ref, bgt_ref, zm_ref, zr_ref, zg_ref, zgt_ref):
    h = _ln(x_ref[0], LN_EPS) * (1.0 + sc_ref[0]) + sh_ref[0]
    hb = h.astype(BF16)
    zm_ref[0] = jnp.dot(hb, wm_ref[...], preferred_element_type=F32) + bm_ref[...]
    zr_ref[0] = jnp.dot(hb, wr_ref[...], preferred_element_type=F32) + br_ref[...]
    zg_ref[0] = jnp.dot(h, wg_ref[...], preferred_element_type=F32, precision=HIGHEST) + bg_ref[...]
    zgt_ref[0] = lax.dot_general(wgt_ref[...], h, (((1,), (1,)), ((), ())),
                                 preferred_element_type=F32, precision=HIGHEST) + bgt_ref[...]


def _inproj(x, sh, sc, wm, bm, wr, br, wg, bg, wgt, bgt, tm):
    B, T, D = x.shape
    wm_n, wr_n = wm.shape[1], wr.shape[1]
    full = lambda a: pl.BlockSpec(a.shape, lambda b, i: (0,) * a.ndim)
    return pl.pallas_call(
        _inproj_kernel,
        grid=(B, T // tm),
        in_specs=[pl.BlockSpec((1, tm, D), lambda b, i: (b, i, 0)),
                  pl.BlockSpec((1, 1, D), lambda b, i: (b, 0, 0)),
                  pl.BlockSpec((1, 1, D), lambda b, i: (b, 0, 0)),
                  full(wm), full(bm), full(wr), full(br), full(wg), full(bg), full(wgt), full(bgt)],
        out_specs=[pl.BlockSpec((1, tm, wm_n), lambda b, i: (b, i, 0)),
                   pl.BlockSpec((1, tm, wr_n), lambda b, i: (b, i, 0)),
                   pl.BlockSpec((1, tm, LANES), lambda b, i: (b, i, 0)),
                   pl.BlockSpec((1, SUBLANES, tm), lambda b, i: (b, 0, i))],
        out_shape=[jax.ShapeDtypeStruct((B, T, wm_n), F32),
                   jax.ShapeDtypeStruct((B, T, wr_n), F32),
                   jax.ShapeDtypeStruct((B, T, LANES), F32),
                   jax.ShapeDtypeStruct((B, SUBLANES, T), F32)],
        compiler_params=pltpu.CompilerParams(dimension_semantics=("parallel", "parallel"),
                                             vmem_limit_bytes=VMEM_LIMIT),
        name="inproj",
    )(x, sh, sc, wm, bm, wr, br, wg, bg, wgt, bgt)


def _head_norm(t):
    return _ln(t, HEAD_EPS)


def _mixers_kernel(zm_ref, zr_ref, zg_ref, zgt_ref, cos_ref, sin_ref, dmat_ref, inter_ref,
                   wend_ref, decay_ref, hg_ref, c0_ref, n0_ref, m0_ref, s0_ref,
                   y_ref, c_ref, n_ref, m_ref, s_ref, *, chunk, t_valid):
    L = chunk
    W = N_HEADS * DH

    @pl.when(pl.program_id(1) == 0)
    def _():
        c_ref[...] = c0_ref[...]
        n_ref[...] = n0_ref[...]
        m_ref[...] = m0_ref[...]
        s_ref[...] = s0_ref[...]

    row = lax.broadcasted_iota(jnp.int32, (L, L), 0)
    col = lax.broadcasted_iota(jnp.int32, (L, L), 1)
    tri = col <= row
    zg = zg_ref[0]
    zgt = zgt_ref[0]
    hg = hg_ref[...]
    if t_valid < L:
        valid_col = lax.broadcasted_iota(jnp.int32, (L, 1), 0) < t_valid
        valid_row = lax.broadcasted_iota(jnp.int32, (1, L), 1) < t_valid
    scale = DH ** -0.5

    for h in range(N_HEADS):
        lo, hi = h * DH, (h + 1) * DH
        q = zm_ref[0, :, lo:hi]
        k = zm_ref[0, :, W + lo:W + hi] * scale
        v = zm_ref[0, :, 2 * W + lo:2 * W + hi]
        og = zm_ref[0, :, 3 * W + lo:3 * W + hi]
        ig_col = zg[:, h:h + 1]
        lf_col = _log_sigmoid(zg[:, N_HEADS + h:N_HEADS + h + 1])
        ig_row = zgt[h:h + 1, :]
        lf_row = _log_sigmoid(zgt[N_HEADS + h:N_HEADS + h + 1, :])
        if t_valid < L:
            ig_col = jnp.where(valid_col, ig_col, MASKED_GATE)
            ig_row = jnp.where(valid_row, ig_row, MASKED_GATE)
            lf_col = jnp.where(valid_col, lf_col, 0.0)
            lf_row = jnp.where(valid_row, lf_row, 0.0)
        C = c_ref[0, h]
        n_row = n_ref[0, h, 0:1, :]
        m_prev = m_ref[0, h, 0:1, 0:1]

        b_col = jnp.sum(jnp.where(tri, lf_row, 0.0), axis=1, keepdims=True)
        b_row = jnp.sum(jnp.where(row <= col, lf_col, 0.0), axis=0, keepdims=True)
        a_col = ig_col - b_col
        a_row = ig_row - b_row
        cm_col = jnp.max(jnp.where(tri, a_row, -jnp.inf), axis=1, keepdims=True)
        M_col = jnp.maximum(m_prev, cm_col)
        dm = jnp.exp(jnp.where(tri, a_row - M_col, -jnp.inf))
        inter = jnp.exp(m_prev - M_col)
        s = _dot_nt(q, k) * dm
        num = inter * _dot(q, C) + _dot(s, v)
        den = inter * jnp.sum(q * n_row, axis=1, keepdims=True) + jnp.sum(s, axis=1, keepdims=True)
        hh = num / jnp.maximum(jnp.abs(den), jnp.exp(-(b_col + M_col)))
        M_end = M_col[L - 1:L, :]
        w_end = jnp.exp(a_col - M_end)
        decay = jnp.exp(m_prev - M_end)
        kw = k * w_end
        c_ref[0, h] = decay * C + _dot_tn(kw, v)
        n_new = decay * n_row + jnp.sum(kw, axis=0, keepdims=True)
        n_ref[0, h] = jnp.broadcast_to(n_new, (SUBLANES, DH))
        m_new = b_col[L - 1:L, :] + M_end
        m_ref[0, h] = jnp.broadcast_to(m_new, (SUBLANES, DH))
        out_a = _head_norm(hh) * jax.nn.sigmoid(og)
        y_ref[0, :, lo:hi] = out_a * hg[:, lo:hi]

        cos2 = cos_ref[...]
        sin2 = sin_ref[...]
        rq = zr_ref[0, :, lo:hi]
        rk = zr_ref[0, :, W + lo:W + hi]
        rv = zr_ref[0, :, 2 * W + lo:2 * W + hi]
        rg = zr_ref[0, :, 3 * W + lo:3 * W + hi]
        qb = rq * cos2 + pltpu.roll(rq, DH // 2, axis=1) * sin2
        kb = (rk * cos2 + pltpu.roll(rk, DH // 2, axis=1) * sin2) * scale
        S = s_ref[0, h]
        sr = _dot_nt(qb, kb) * dmat_ref[h]
        o = _dot(sr, rv) + inter_ref[h] * _dot(qb, S)
        s_ref[0, h] = decay_ref[h] * S + _dot_tn(kb * wend_ref[h], rv)
        out_b = _head_norm(o) * (rg * jax.nn.sigmoid(rg))
        y_ref[0, :, W + lo:W + hi] = out_b * hg[:, W + lo:W + hi]


def _mixers(zm, zr, zg, zgt, cos2, sin2, dmat, inter, wend, decay, head_g, C0, n0, m0, S0,
            chunk, t_valid):
    B, T, _ = zm.shape
    nC = T // chunk
    D = 2 * N_HEADS * DH
    full = lambda a: pl.BlockSpec(a.shape, lambda b, c: (0,) * a.ndim)
    st4 = pl.BlockSpec((1, N_HEADS, DH, DH), lambda b, c: (b, 0, 0, 0))
    st3 = pl.BlockSpec((1, N_HEADS, SUBLANES, DH), lambda b, c: (b, 0, 0, 0))
    return pl.pallas_call(
        functools.partial(_mixers_kernel, chunk=chunk, t_valid=t_valid),
        grid=(B, nC),
        in_specs=[pl.BlockSpec((1, chunk, zm.shape[2]), lambda b, c: (b, c, 0)),
                  pl.BlockSpec((1, chunk, zr.shape[2]), lambda b, c: (b, c, 0)),
                  pl.BlockSpec((1, chunk, LANES), lambda b, c: (b, c, 0)),
                  pl.BlockSpec((1, SUBLANES, chunk), lambda b, c: (b, 0, c)),
                  pl.BlockSpec((chunk, DH), lambda b, c: (c, 0)),
                  pl.BlockSpec((chunk, DH), lambda b, c: (c, 0)),
                  full(dmat), full(inter), full(wend), full(decay), full(head_g),
                  st4, st3, st3, st4],
        out_specs=[pl.BlockSpec((1, chunk, D), lambda b, c: (b, c, 0)), st4, st3, st3, st4],
        out_shape=[jax.ShapeDtypeStruct((B, T, D), F32),
                   jax.ShapeDtypeStruct(C0.shape, F32),
                   jax.ShapeDtypeStruct(n0.shape, F32),
                   jax.ShapeDtypeStruct(m0.shape, F32),
                   jax.ShapeDtypeStruct(S0.shape, F32)],
        compiler_params=pltpu.CompilerParams(dimension_semantics=("parallel", "arbitrary"),
                                             vmem_limit_bytes=VMEM_LIMIT),
        name="mixers",
    )(zm, zr, zg, zgt, cos2, sin2, dmat, inter, wend, decay, head_g, C0, n0, m0, S0)


def _topk_rows(s, k, idx_bound):
    n = s.shape[0]
    iota = lax.broadcasted_iota(jnp.int32, s.shape, 0)
    vals, idxs = [], []
    for _ in range(k):
        m = jnp.max(s, axis=0, keepdims=True)
        i = jnp.min(jnp.where(s == m, iota, idx_bound), axis=0, keepdims=True)
        vals.append(m)
        idxs.append(i)
        s = jnp.where(iota == i, -jnp.inf, s)
    return vals, idxs


def _post_kernel(x_ref, y_ref, g1_ref, sh2_ref, sc2_ref, wout_ref, l1g_ref, l1b_ref, wpq_ref,
                 keys_ref, x1_ref, h2_ref, e_ref, g_ref, *, alpha):
    mix = _dot(y_ref[0], wout_ref[...])
    x1 = _ln(alpha * x_ref[0] + g1_ref[0] * mix, LN_EPS) * l1g_ref[...] + l1b_ref[...]
    x1_ref[0] = x1
    h2 = _ln(x1, LN_EPS) * (1.0 + sc2_ref[0]) + sh2_ref[0]
    h2_ref[0] = h2
    qall = _dot(h2, wpq_ref[...])
    nk = keys_ref.shape[1]
    K = PEER_TOPK
    e_rows, g_rows = [], []
    for h in range(PEER_HEADS):
        sv, si = [], []
        for c in range(2):
            lo = (h * 2 + c) * DH
            st = _dot_nt(keys_ref[c], qall[:, lo:lo + DH])
            v, i = _topk_rows(st, K, nk)
            sv.append(v)
            si.append(i)
        sv1 = jnp.concatenate(sv[1], axis=0)
        si1 = jnp.concatenate(si[1], axis=0)
        cand = jnp.concatenate([sv[0][i] + sv1 for i in range(K)], axis=0)
        cidx = jnp.concatenate([si[0][i] * nk + si1 for i in range(K)], axis=0)
        iota = lax.broadcasted_iota(jnp.int32, cand.shape, 0)
        tv, te = [], []
        for _ in range(K):
            m = jnp.max(cand, axis=0, keepdims=True)
            fi = jnp.min(jnp.where(cand == m, iota, K * K), axis=0, keepdims=True)
            sel = iota == fi
            te.append(jnp.sum(jnp.where(sel, cidx, 0), axis=0, keepdims=True))
            tv.append(m)
            cand = jnp.where(sel, -jnp.inf, cand)
        tv = jnp.concatenate(tv, axis=0)
        ex = jnp.exp(tv - tv[0:1, :])
        g_rows.append(ex / jnp.sum(ex, axis=0, keepdims=True))
        e_rows.append(jnp.concatenate(te, axis=0))
    e_all = jnp.concatenate(e_rows, axis=0) * ROW_SUB
    g_all = jnp.concatenate(g_rows, axis=0)
    e_ref[0] = pltpu.bitcast(pltpu.bitcast(e_all, F32).T, jnp.int32)
    g_ref[0] = g_all.T


def _post(x, ymix, g1, sh2, sc2, w_out, ln1_g, ln1_b, w_pq, keys, alpha, tm):
    B, T, D = x.shape
    npair = PEER_HEADS * PEER_TOPK
    full = lambda a: pl.BlockSpec(a.shape, lambda b, i: (0,) * a.ndim)
    tok = lambda w: pl.BlockSpec((1, tm, w), lambda b, i: (b, i, 0))
    bat = pl.BlockSpec((1, 1, D), lambda b, i: (b, 0, 0))
    return pl.pallas_call(
        functools.partial(_post_kernel, alpha=alpha),
        grid=(B, T // tm),
        in_specs=[tok(D), tok(D), bat, bat, bat, full(w_out), full(ln1_g), full(ln1_b),
                  full(w_pq), full(keys)],
        out_specs=[tok(D), tok(D), tok(npair), tok(npair)],
        out_shape=[jax.ShapeDtypeStruct((B, T, D), F32),
                   jax.ShapeDtypeStruct((B, T, D), F32),
                   jax.ShapeDtypeStruct((B, T, npair), jnp.int32),
                   jax.ShapeDtypeStruct((B, T, npair), F32)],
        compiler_params=pltpu.CompilerParams(dimension_semantics=("parallel", "parallel"),
                                             vmem_limit_bytes=VMEM_LIMIT),
        name="post",
    )(x, ymix, g1, sh2, sc2, w_out, ln1_g, ln1_b, w_pq, keys)


def _unpack_pair(words):
    lo = pltpu.bitcast(words << 16, F32)
    hi = pltpu.bitcast(words & jnp.int32(-65536), F32)
    return lo, hi


def _gelu_exact(x):
    return 0.5 * x * (1.0 + lax.erf(x * (1.0 / math.sqrt(2.0))))


def _peer_u_kernel(e_ref, x_ref, g_ref, tbl_ref, w_ref, slot_ref, *, tile):
    npair = PEER_HEADS * PEER_TOPK
    ones = jnp.ones((SUBLANES, LANES), BF16)

    def token(t, carry):
        x8 = x_ref[pl.ds(pl.multiple_of(t * SUBLANES, SUBLANES), SUBLANES), :]
        xlo = x8[0:ROW_SUB]
        xhi = x8[ROW_SUB:2 * ROW_SUB]
        for p in range(npair):
            r0 = pl.multiple_of(e_ref[t * npair + p], ROW_SUB)
            lo, hi = _unpack_pair(tbl_ref[pl.ds(r0, ROW_SUB), :])
            slot_ref[pl.ds(p, ROW_SUB, stride=SLOT_STRIDE), :] = lo * xlo + hi * xhi
        part = slot_ref[pl.ds(0, npair), :]
        for j in range(1, ROW_SUB):
            part = part + slot_ref[pl.ds(j * SLOT_STRIDE, npair), :]
        p_hi = part.astype(BF16)
        p_lo = (part - p_hi.astype(F32)).astype(BF16)
        dn = (((1,), (1,)), ((), ()))
        act = (lax.dot_general(ones, p_hi, dn, preferred_element_type=F32)
               + lax.dot_general(ones, p_lo, dn, preferred_element_type=F32))[0:1, :]
        w_ref[t] = g_ref[t] * _gelu_exact(act)
        return carry

    lax.fori_loop(0, tile, token, 0)


def _peer_u(eflat, x8, gw3, table, tile):
    n = gw3.shape[0]
    npair = gw3.shape[2]
    slot_rows = (ROW_SUB - 1) * SLOT_STRIDE + npair
    slot_rows = -(-slot_rows // SUBLANES) * SUBLANES
    return pl.pallas_call(
        functools.partial(_peer_u_kernel, tile=tile),
        grid=(n // tile,),
        in_specs=[pl.BlockSpec((tile * npair,), lambda i: (i,), memory_space=pltpu.SMEM),
                  pl.BlockSpec((tile * SUBLANES, LANES), lambda i: (i, 0)),
                  pl.BlockSpec((tile, 1, npair), lambda i: (i, 0, 0)),
                  pl.BlockSpec(memory_space=pltpu.VMEM)],
        out_specs=pl.BlockSpec((tile, 1, npair), lambda i: (i, 0, 0)),
        out_shape=jax.ShapeDtypeStruct((n, 1, npair), F32),
        scratch_shapes=[pltpu.VMEM((slot_rows, LANES), F32)],
        compiler_params=pltpu.CompilerParams(dimension_semantics=("arbitrary",),
                                             vmem_limit_bytes=VMEM_LIMIT),
        name="peer_u",
    )(eflat, x8, gw3, table)


def _peer_v_kernel(e_ref, w_ref, tbl_ref, o_ref, *, tile):
    npair = PEER_HEADS * PEER_TOPK
    nacc = 4

    def token(t, carry):
        acc_lo = [jnp.zeros((ROW_SUB, LANES), F32) for _ in range(nacc)]
        acc_hi = [jnp.zeros((ROW_SUB, LANES), F32) for _ in range(nacc)]
        for p in range(npair):
            r0 = pl.multiple_of(e_ref[t * npair + p], ROW_SUB)
            w = w_ref[t * npair + p]
            lo, hi = _unpack_pair(tbl_ref[pl.ds(r0, ROW_SUB), :])
            acc_lo[p % nacc] = acc_lo[p % nacc] + w * lo
            acc_hi[p % nacc] = acc_hi[p % nacc] + w * hi
        lo = (acc_lo[0] + acc_lo[1]) + (acc_lo[2] + acc_lo[3])
        hi = (acc_hi[0] + acc_hi[1]) + (acc_hi[2] + acc_hi[3])
        o_ref[pl.ds(pl.multiple_of(t * SUBLANES, SUBLANES), SUBLANES), :] = jnp.concatenate([lo, hi], axis=0)
        return carry

    lax.fori_loop(0, tile, token, 0)


def _peer_v(eflat, wflat, table, n, tile):
    npair = PEER_HEADS * PEER_TOPK
    return pl.pallas_call(
        functools.partial(_peer_v_kernel, tile=tile),
        grid=(n // tile,),
        in_specs=[pl.BlockSpec((tile * npair,), lambda i: (i,), memory_space=pltpu.SMEM),
                  pl.BlockSpec((tile * npair,), lambda i: (i,), memory_space=pltpu.SMEM),
                  pl.BlockSpec(memory_space=pltpu.VMEM)],
        out_specs=pl.BlockSpec((tile * SUBLANES, LANES), lambda i: (i, 0)),
        out_shape=jax.ShapeDtypeStruct((n * SUBLANES, LANES), F32),
        compiler_params=pltpu.CompilerParams(dimension_semantics=("arbitrary",),
                                             vmem_limit_bytes=VMEM_LIMIT),
        name="peer_v",
    )(eflat, wflat, table)


def _pack_table(t):
    bits = lax.bitcast_convert_type(t.astype(BF16), jnp.uint16).astype(jnp.uint32)
    words = bits[:, :ROW_WORDS] | (bits[:, ROW_WORDS:] << 16)
    return lax.bitcast_convert_type(words, jnp.int32).reshape(t.shape[0] * ROW_SUB, LANES)


def _final_kernel(x_ref, p_ref, g2_ref, lg_ref, lb_ref, o_ref, *, alpha):
    o_ref[0] = _ln(alpha * x_ref[0] + g2_ref[0] * p_ref[0], LN_EPS) * lg_ref[...] + lb_ref[...]


def _final(x1, peer, g2, ln2_g, ln2_b, alpha, tm):
    B, T, D = x1.shape
    tok = pl.BlockSpec((1, tm, D), lambda b, i: (b, i, 0))
    vec = pl.BlockSpec((1, D), lambda b, i: (0, 0))
    return pl.pallas_call(
        functools.partial(_final_kernel, alpha=alpha),
        grid=(B, T // tm),
        in_specs=[tok, tok, pl.BlockSpec((1, 1, D), lambda b, i: (b, 0, 0)), vec, vec],
        out_specs=tok,
        out_shape=jax.ShapeDtypeStruct((B, T, D), F32),
        compiler_params=pltpu.CompilerParams(dimension_semantics=("parallel", "parallel")),
        name="final",
    )(x1, peer, g2, ln2_g, ln2_b)


def _retention_consts(chunk, t_valid):
    lg = jnp.log1p(-(2.0 ** (-5.0 - jnp.arange(N_HEADS, dtype=F32))))
    pos = jnp.arange(chunk, dtype=F32)
    diff = pos[:, None] - pos[None, :]
    dmat = jnp.where(diff >= 0, jnp.exp(lg[:, None, None] * jnp.maximum(diff, 0.0)), 0.0)
    inter = jnp.exp(lg[:, None] * (pos + 1.0))[..., None]
    wend = jnp.where(pos < t_valid, jnp.exp(lg[:, None] * (t_valid - 1.0 - pos)), 0.0)[..., None]
    decay = jnp.exp(lg * t_valid)[:, None, None]
    return dmat, inter, wend, decay


def _rotary_tables(T, pos0):
    half = DH // 2
    inv = ROPE_BASE ** (-jnp.arange(half, dtype=F32) / half)
    ang = (pos0 + jnp.arange(T, dtype=F32))[:, None] * inv[None, :]
    cos, sin = jnp.cos(ang), jnp.sin(ang)
    return jnp.concatenate([cos, cos], axis=-1), jnp.concatenate([-sin, sin], axis=-1)


def _layer(x, c, pos0, t_valid, C0, n0, m0, S0, p, alpha):
    B, T, D = x.shape
    W = N_HEADS * DH
    mod = _ada(c, p["w_ada"], p["b_ada"])
    sh1, sc1, g1, sh2, sc2, g2 = [m[:, None, :] for m in jnp.split(mod, 6, axis=-1)]

    w_in, b_in = p["w_in"], p["b_in"]
    g_off = 4 * W
    r_off = g_off + 2 * N_HEADS
    wm = w_in[:, :g_off].astype(BF16)
    wr = w_in[:, r_off:].astype(BF16)
    wg = jnp.pad(w_in[:, g_off:r_off], ((0, 0), (0, LANES - 2 * N_HEADS)))
    bg = jnp.pad(b_in[g_off:r_off], (0, LANES - 2 * N_HEADS))[None, :]
    wgt = w_in[:, g_off:r_off].T
    bgt = b_in[g_off:r_off][:, None]
    tm = min(T, 512)
    zm, zr, zg, zgt = _inproj(x, sh1, sc1, wm, b_in[None, :g_off], wr, b_in[None, r_off:],
                              wg, bg, wgt, bgt, tm)

    chunk = min(T, MIX_CHUNK)
    tv = min(t_valid, chunk) if T == chunk else chunk
    cos2, sin2 = _rotary_tables(T, pos0)
    dmat, inter, wend, decay = _retention_consts(chunk, tv)
    bc = lambda a: jnp.broadcast_to(a[..., None, :], a.shape[:-1] + (SUBLANES, a.shape[-1]))
    n0b = bc(n0.astype(F32))
    m0b = jnp.broadcast_to(m0.astype(F32)[..., None, None], m0.shape + (SUBLANES, DH))
    ymix, C1, n1, m1, S1 = _mixers(zm, zr, zg, zgt, cos2, sin2, dmat, inter, wend, decay,
                                   p["head_g"][None, :], C0.astype(F32), n0b, m0b, S0.astype(F32),
                                   chunk, tv)
    n1 = n1[:, :, 0, :]
    m1 = m1[:, :, 0, 0]

    x1, h2, eidx, gw = _post(x, ymix, g1, sh2, sc2, p["w_out"].astype(BF16), p["ln1_g"][None, :],
                             p["ln1_b"][None, :], p["w_pq"].astype(BF16), p["sub_keys"], alpha,
                             POST_TILE)
    n = B * T
    npair = PEER_HEADS * PEER_TOPK
    tile = min(n, PEER_TILE)
    eflat = eidx.reshape(n * npair)
    wts = _peer_u(eflat, h2.reshape(n * SUBLANES, LANES), gw.reshape(n, 1, npair),
                  _pack_table(p["expert_u"]), tile)
    peer = _peer_v(eflat, wts.reshape(n * npair), _pack_table(p["expert_v"]), n, tile)
    y = _final(x1, peer.reshape(B, T, D), g2, p["ln2_g"][None, :], p["ln2_b"][None, :], alpha,
               min(T, 512))
    return y, (C1, n1, m1, S1)


def kernel(x_prompt, x_sample, c_prompt, c_sample, state_mlstm_C, state_mlstm_n, state_mlstm_m,
           state_ret_S, w_ada, b_ada, w_in, b_in, head_g, w_out, ln1_g, ln1_b, w_pq, sub_keys,
           expert_u, expert_v, ln2_g, ln2_b):
    depth = w_ada.shape[0]
    alpha = (2.0 * depth) ** 0.25
    params = dict(w_ada=w_ada, b_ada=b_ada, w_in=w_in, b_in=b_in, head_g=head_g, w_out=w_out,
                  ln1_g=ln1_g, ln1_b=ln1_b, w_pq=w_pq, sub_keys=sub_keys, expert_u=expert_u,
                  expert_v=expert_v, ln2_g=ln2_g, ln2_b=ln2_b)

    def run(x, c, pos0, C0, n0, m0, S0):
        B, T, D = x.shape
        t_valid = T
        if T % LANES:
            x = jnp.pad(x, ((0, 0), (0, LANES - T % LANES), (0, 0)))
        Cs, ns, ms, Ss = [], [], [], []
        for l in range(depth):
            pl_ = {k: v[l] for k, v in params.items()}
            x, (C1, n1, m1, S1) = _layer(x, c, pos0, t_valid, C0[l], n0[l], m0[l], S0[l], pl_, alpha)
            Cs.append(C1); ns.append(n1); ms.append(m1); Ss.append(S1)
        return x[:, :T], jnp.stack(Cs), jnp.stack(ns), jnp.stack(ms), jnp.stack(Ss)

    B = x_prompt.shape[0]
    H = state_mlstm_C.shape[2]
    zC = jnp.zeros((depth, B, H, DH, DH), F32)
    zn = jnp.zeros((depth, B, H, DH), F32)
    zm = jnp.zeros((depth, B, H), F32)
    zS = jnp.zeros((depth, B, H, DH, DH), F32)
    past_len = 1024
    y_p, C_p, n_p, m_p, S_p = run(x_prompt, c_prompt, 0, zC, zn, zm, zS)
    y_s, C_s, n_s, m_s, S_s = run(x_sample, c_sample, past_len, state_mlstm_C, state_mlstm_n,
                                  state_mlstm_m, state_ret_S)
    return (y_p, y_s, C_p, n_p, m_p, S_p, C_s, n_s, m_s, S_s)
```

```python
import functools
import math

import jax
import jax.numpy as jnp
from jax import lax
from jax.experimental import pallas as pl
from jax.experimental.pallas import tpu as pltpu

F32 = jnp.float32
BF16 = jnp.bfloat16
HIGHEST = lax.Precision.HIGHEST

LN_EPS = 1e-5
HEAD_EPS = 1e-6
ROPE_BASE = 10000.0
N_HEADS = 4
DH = 128
LANES = 128
SUBLANES = 8
PEER_HEADS = 8
PEER_TOPK = 16
MIX_CHUNK = 256
POST_TILE = 128
PEER_TILE = 64
ROW_WORDS = 512
ROW_SUB = ROW_WORDS // LANES
SLOT_STRIDE = 129
MASKED_GATE = -1e30
VMEM_LIMIT = 56 * 1024 * 1024


def _ln(x, eps):
    mu = jnp.mean(x, axis=-1, keepdims=True)
    xc = x - mu
    var = jnp.mean(xc * xc, axis=-1, keepdims=True)
    return xc * lax.rsqrt(var + eps)


def _dot(a, b):
    return jnp.dot(a.astype(BF16), b.astype(BF16), preferred_element_type=F32)


def _dot_nt(a, b):
    return lax.dot_general(a.astype(BF16), b.astype(BF16), (((1,), (1,)), ((), ())),
                           preferred_element_type=F32)


def _dot_tn(a, b):
    return lax.dot_general(a.astype(BF16), b.astype(BF16), (((0,), (0,)), ((), ())),
                           preferred_element_type=F32)


def _log_sigmoid(x):
    return jnp.minimum(x, 0.0) - jnp.log1p(jnp.exp(-jnp.abs(x)))


def _ada_kernel(c_ref, w_ref, b_ref, o_ref):
    c = c_ref[...]
    s = c * jax.nn.sigmoid(c)
    o_ref[...] = jnp.dot(s, w_ref[...], preferred_element_type=F32, precision=HIGHEST) + b_ref[...]


def _ada(c, w_ada, b_ada):
    B, D = c.shape
    n = w_ada.shape[1]
    return pl.pallas_call(
        _ada_kernel,
        grid=(n // D,),
        in_specs=[pl.BlockSpec((B, D), lambda j: (0, 0)),
                  pl.BlockSpec((D, D), lambda j: (0, j)),
                  pl.BlockSpec((1, D), lambda j: (0, j))],
        out_specs=pl.BlockSpec((B, D), lambda j: (0, j)),
        out_shape=jax.ShapeDtypeStruct((B, n), F32),
        name="ada",
    )(c, w_ada, b_ada.reshape(1, n))


def _inproj_kernel(x_ref, sh_ref, sc_ref, wm_ref, bm_ref, wr_ref, br_ref, wg_ref, bg_ref,
                   wgt_ref, bgt_ref, zm_ref, zr_ref, zg_ref, zgt_ref):
    h = _ln(x_ref[0], LN_EPS) * (1.0 + sc_ref[0]) + sh_ref[0]
    hb = h.astype(BF16)
    zm_ref[0] = jnp.dot(hb, wm_ref[...], preferred_element_type=F32) + bm_ref[...]
    zr_ref[0] = jnp.dot(hb, wr_ref[...], preferred_element_type=F32) + br_ref[...]
    zg_ref[0] = jnp.dot(h, wg_ref[...], preferred_element_type=F32, precision=HIGHEST) + bg_ref[...]
    zgt_ref[0] = lax.dot_general(wgt_ref[...], h, (((1,), (1,)), ((), ())),
                                 preferred_element_type=F32, precision=HIGHEST) + bgt_ref[...]


def _inproj(x, sh, sc, wm, bm, wr, br, wg, bg, wgt, bgt, tm):
    B, T, D = x.shape
    wm_n, wr_n = wm.shape[1], wr.shape[1]
    full = lambda a: pl.BlockSpec(a.shape, lambda b, i: (0,) * a.ndim)
    return pl.pallas_call(
        _inproj_kernel,
        grid=(B, T // tm),
        in_specs=[pl.BlockSpec((1, tm, D), lambda b, i: (b, i, 0)),
                  pl.BlockSpec((1, 1, D), lambda b, i: (b, 0, 0)),
                  pl.BlockSpec((1, 1, D), lambda b, i: (b, 0, 0)),
                  full(wm), full(bm), full(wr), full(br), full(wg), full(bg), full(wgt), full(bgt)],
        out_specs=[pl.BlockSpec((1, tm, wm_n), lambda b, i: (b, i, 0)),
                   pl.BlockSpec((1, tm, wr_n), lambda b, i: (b, i, 0)),
                   pl.BlockSpec((1, tm, LANES), lambda b, i: (b, i, 0)),
                   pl.BlockSpec((1, SUBLANES, tm), lambda b, i: (b, 0, i))],
        out_shape=[jax.ShapeDtypeStruct((B, T, wm_n), F32),
                   jax.ShapeDtypeStruct((B, T, wr_n), F32),
                   jax.ShapeDtypeStruct((B, T, LANES), F32),
                   jax.ShapeDtypeStruct((B, SUBLANES, T), F32)],
        compiler_params=pltpu.CompilerParams(dimension_semantics=("parallel", "parallel"),
                                             vmem_limit_bytes=VMEM_LIMIT),
        name="inproj",
    )(x, sh, sc, wm, bm, wr, br, wg, bg, wgt, bgt)


def _head_norm(t):
    return _ln(t, HEAD_EPS)


def _mixers_kernel(zm_ref, zr_ref, zg_ref, zgt_ref, cos_ref, sin_ref, dmat_ref, inter_ref,
                   wend_ref, decay_ref, hg_ref, c0_ref, n0_ref, m0_ref, s0_ref,
                   y_ref, c_ref, n_ref, m_ref, s_ref, *, chunk, t_valid):
    L = chunk
    W = N_HEADS * DH

    @pl.when(pl.program_id(1) == 0)
    def _():
        c_ref[...] = c0_ref[...]
        n_ref[...] = n0_ref[...]
        m_ref[...] = m0_ref[...]
        s_ref[...] = s0_ref[...]

    row = lax.broadcasted_iota(jnp.int32, (L, L), 0)
    col = lax.broadcasted_iota(jnp.int32, (L, L), 1)
    tri = col <= row
    zg = zg_ref[0]
    zgt = zgt_ref[0]
    hg = hg_ref[...]
    if t_valid < L:
        valid_col = lax.broadcasted_iota(jnp.int32, (L, 1), 0) < t_valid
        valid_row = lax.broadcasted_iota(jnp.int32, (1, L), 1) < t_valid
    scale = DH ** -0.5

    for h in range(N_HEADS):
        lo, hi = h * DH, (h + 1) * DH
        q = zm_ref[0, :, lo:hi]
        k = zm_ref[0, :, W + lo:W + hi] * scale
        v = zm_ref[0, :, 2 * W + lo:2 * W + hi]
        og = zm_ref[0, :, 3 * W + lo:3 * W + hi]
        ig_col = zg[:, h:h + 1]
        lf_col = _log_sigmoid(zg[:, N_HEADS + h:N_HEADS + h + 1])
        ig_row = zgt[h:h + 1, :]
        lf_row = _log_sigmoid(zgt[N_HEADS + h:N_HEADS + h + 1, :])
        if t_valid < L:
            ig_col = jnp.where(valid_col, ig_col, MASKED_GATE)
            ig_row = jnp.where(valid_row, ig_row, MASKED_GATE)
            lf_col = jnp.where(valid_col, lf_col, 0.0)
            lf_row = jnp.where(valid_row, lf_row, 0.0)
        C = c_ref[0, h]
        n_row = n_ref[0, h, 0:1, :]
        m_prev = m_ref[0, h, 0:1, 0:1]

        b_col = jnp.sum(jnp.where(tri, lf_row, 0.0), axis=1, keepdims=True)
        b_row = jnp.sum(jnp.where(row <= col, lf_col, 0.0), axis=0, keepdims=True)
        a_col = ig_col - b_col
        a_row = ig_row - b_row
        cm_col = jnp.max(jnp.where(tri, a_row, -jnp.inf), axis=1, keepdims=True)
        M_col = jnp.maximum(m_prev, cm_col)
        dm = jnp.exp(jnp.where(tri, a_row - M_col, -jnp.inf))
        inter = jnp.exp(m_prev - M_col)
        s = _dot_nt(q, k) * dm
        num = inter * _dot(q, C) + _dot(s, v)
        den = inter * jnp.sum(q * n_row, axis=1, keepdims=True) + jnp.sum(s, axis=1, keepdims=True)
        hh = num / jnp.maximum(jnp.abs(den), jnp.exp(-(b_col + M_col)))
        M_end = M_col[L - 1:L, :]
        w_end = jnp.exp(a_col - M_end)
        decay = jnp.exp(m_prev - M_end)
        kw = k * w_end
        c_ref[0, h] = decay * C + _dot_tn(kw, v)
        n_new = decay * n_row + jnp.sum(kw, axis=0, keepdims=True)
        n_ref[0, h] = jnp.broadcast_to(n_new, (SUBLANES, DH))
        m_new = b_col[L - 1:L, :] + M_end
        m_ref[0, h] = jnp.broadcast_to(m_new, (SUBLANES, DH))
        out_a = _head_norm(hh) * jax.nn.sigmoid(og)
        y_ref[0, :, lo:hi] = out_a * hg[:, lo:hi]

        cos2 = cos_ref[...]
        sin2 = sin_ref[...]
        rq = zr_ref[0, :, lo:hi]
        rk = zr_ref[0, :, W + lo:W + hi]
        rv = zr_ref[0, :, 2 * W + lo:2 * W + hi]
        rg = zr_ref[0, :, 3 * W + lo:3 * W + hi]
        qb = rq * cos2 + pltpu.roll(rq, DH // 2, axis=1) * sin2
        kb = (rk * cos2 + pltpu.roll(rk, DH // 2, axis=1) * sin2) * scale
        S = s_ref[0, h]
        sr = _dot_nt(qb, kb) * dmat_ref[h]
        o = _dot(sr, rv) + inter_ref[h] * _dot(qb, S)
        s_ref[0, h] = decay_ref[h] * S + _dot_tn(kb * wend_ref[h], rv)
        out_b = _head_norm(o) * (rg * jax.nn.sigmoid(rg))
        y_ref[0, :, W + lo:W + hi] = out_b * hg[:, W + lo:W + hi]


def _mixers(zm, zr, zg, zgt, cos2, sin2, dmat, inter, wend, decay, head_g, C0, n0, m0, S0,
            chunk, t_valid):
    B, T, _ = zm.shape
    nC = T // chunk
    D = 2 * N_HEADS * DH
    full = lambda a: pl.BlockSpec(a.shape, lambda b, c: (0,) * a.ndim)
    st4 = pl.BlockSpec((1, N_HEADS, DH, DH), lambda b, c: (b, 0, 0, 0))
    st3 = pl.BlockSpec((1, N_HEADS, SUBLANES, DH), lambda b, c: (b, 0, 0, 0))
    return pl.pallas_call(
        functools.partial(_mixers_kernel, chunk=chunk, t_valid=t_valid),
        grid=(B, nC),
        in_specs=[pl.BlockSpec((1, chunk, zm.shape[2]), lambda b, c: (b, c, 0)),
                  pl.BlockSpec((1, chunk, zr.shape[2]), lambda b, c: (b, c, 0)),
                  pl.BlockSpec((1, chunk, LANES), lambda b, c: (b, c, 0)),
                  pl.BlockSpec((1, SUBLANES, chunk), lambda b, c: (b, 0, c)),
                  pl.BlockSpec((chunk, DH), lambda b, c: (c, 0)),
                  pl.BlockSpec((chunk, DH), lambda b, c: (c, 0)),
                  full(dmat), full(inter), full(wend), full(decay), full(head_g),
                  st4, st3, st3, st4],
        out_specs=[pl.BlockSpec((1, chunk, D), lambda b, c: (b, c, 0)), st4, st3, st3, st4],
        out_shape=[jax.ShapeDtypeStruct((B, T, D), F32),
                   jax.ShapeDtypeStruct(C0.shape, F32),
                   jax.ShapeDtypeStruct(n0.shape, F32),
                   jax.ShapeDtypeStruct(m0.shape, F32),
                   jax.ShapeDtypeStruct(S0.shape, F32)],
        compiler_params=pltpu.CompilerParams(dimension_semantics=("parallel", "arbitrary"),
                                             vmem_limit_bytes=VMEM_LIMIT),
        name="mixers",
    )(zm, zr, zg, zgt, cos2, sin2, dmat, inter, wend, decay, head_g, C0, n0, m0, S0)


def _argmax_blocks(vals, tags, extras=()):
    v, t = list(vals), list(tags)
    ex = [list(e) for e in extras]
    while len(v) > 1:
        nv, nt, ne = [], [], [[] for _ in ex]
        for a in range(0, len(v) - 1, 2):
            take = v[a] >= v[a + 1]
            nv.append(jnp.where(take, v[a], v[a + 1]))
            nt.append(jnp.where(take, t[a], t[a + 1]))
            for q, e in enumerate(ex):
                ne[q].append(jnp.where(take, e[a], e[a + 1]))
        if len(v) % 2:
            nv.append(v[-1])
            nt.append(t[-1])
            for q, e in enumerate(ex):
                ne[q].append(e[-1])
        v, t, ex = nv, nt, ne
    m = jnp.max(v[0], axis=0, keepdims=True)
    tag = jnp.min(jnp.where(v[0] == m, t[0], jnp.int32(2 ** 30)), axis=0, keepdims=True)
    outs = [jnp.sum(jnp.where(t[0] == tag, e[0], 0), axis=0, keepdims=True) for e in ex]
    return m, tag, outs


def _topk_sorted(s, k):
    n, tok = s.shape
    nb = n // SUBLANES
    sub = lax.broadcasted_iota(jnp.int32, (SUBLANES, tok), 0)
    blocks = [s[b * SUBLANES:(b + 1) * SUBLANES] for b in range(nb)]
    tags = [sub + b * SUBLANES for b in range(nb)]
    v_rows, i_rows = [], []
    v_blk = [jnp.zeros((SUBLANES, tok), F32) for _ in range(k // SUBLANES)]
    i_blk = [jnp.zeros((SUBLANES, tok), jnp.int32) for _ in range(k // SUBLANES)]
    for r in range(k):
        m, i, _ = _argmax_blocks(blocks, tags)
        v_rows.append(m)
        i_rows.append(i)
        at = sub == (r % SUBLANES)
        v_blk[r // SUBLANES] = jnp.where(at, m, v_blk[r // SUBLANES])
        i_blk[r // SUBLANES] = jnp.where(at, i, i_blk[r // SUBLANES])
        blocks = [jnp.where(t == i, -jnp.inf, b) for b, t in zip(blocks, tags)]
    return v_rows, i_rows, v_blk, i_blk


def _product_topk(top0, top1, nk):
    K = PEER_TOPK
    v0_rows, i0_rows, v0_blk, i0_blk = top0
    v1_rows, i1_rows, v1_blk, i1_blk = top1
    tok = v0_rows[0].shape[1]
    sub = lax.broadcasted_iota(jnp.int32, (SUBLANES, tok), 0)
    vals, flats, experts = [], [], []
    for i in range(SUBLANES):
        for jb in range(K // SUBLANES):
            limit = K // (i + 1) - jb * SUBLANES
            if limit <= 0:
                continue
            blk = v0_rows[i] + v1_blk[jb]
            if limit < SUBLANES:
                blk = jnp.where(sub < limit, blk, -jnp.inf)
            vals.append(blk)
            flats.append(sub + (i * K + jb * SUBLANES))
            experts.append(i0_rows[i] * nk + i1_blk[jb])
    vals.append(v0_blk[1] + v1_rows[0])
    flats.append((sub + SUBLANES) * K)
    experts.append(i0_blk[1] * nk + i1_rows[0])
    tv, te = [], []
    for _ in range(K):
        m, f, (e,) = _argmax_blocks(vals, flats, extras=(experts,))
        tv.append(m)
        te.append(e)
        vals = [jnp.where(t == f, -jnp.inf, b) for b, t in zip(vals, flats)]
    return tv, te


def _post_kernel(x_ref, y_ref, g1_ref, sh2_ref, sc2_ref, wout_ref, l1g_ref, l1b_ref, wpq_ref,
                 keys_ref, x1_ref, h2_ref, e_ref, g_ref, *, alpha):
    mix = _dot(y_ref[0], wout_ref[...])
    x1 = _ln(alpha * x_ref[0] + g1_ref[0] * mix, LN_EPS) * l1g_ref[...] + l1b_ref[...]
    x1_ref[0] = x1
    h2 = _ln(x1, LN_EPS) * (1.0 + sc2_ref[0]) + sh2_ref[0]
    h2_ref[0] = h2
    qall = _dot(h2, wpq_ref[...])
    nk = keys_ref.shape[1]
    K = PEER_TOPK
    e_rows, g_rows = [], []
    for h in range(PEER_HEADS):
        tops = []
        for c in range(2):
            lo = (h * 2 + c) * DH
            st = _dot_nt(keys_ref[c], qall[:, lo:lo + DH])
            tops.append(_topk_sorted(st, K))
        tv, te = _product_topk(tops[0], tops[1], nk)
        tv = jnp.concatenate(tv, axis=0)
        ex = jnp.exp(tv - tv[0:1, :])
        g_rows.append(ex / jnp.sum(ex, axis=0, keepdims=True))
        e_rows.append(jnp.concatenate(te, axis=0))
    e_all = jnp.concatenate(e_rows, axis=0) * ROW_SUB
    g_all = jnp.concatenate(g_rows, axis=0)
    e_ref[0] = pltpu.bitcast(pltpu.bitcast(e_all, F32).T, jnp.int32)
    g_ref[0] = g_all.T


def _post(x, ymix, g1, sh2, sc2, w_out, ln1_g, ln1_b, w_pq, keys, alpha, tm):
    B, T, D = x.shape
    npair = PEER_HEADS * PEER_TOPK
    full = lambda a: pl.BlockSpec(a.shape, lambda b, i: (0,) * a.ndim)
    tok = lambda w: pl.BlockSpec((1, tm, w), lambda b, i: (b, i, 0))
    bat = pl.BlockSpec((1, 1, D), lambda b, i: (b, 0, 0))
    return pl.pallas_call(
        functools.partial(_post_kernel, alpha=alpha),
        grid=(B, T // tm),
        in_specs=[tok(D), tok(D), bat, bat, bat, full(w_out), full(ln1_g), full(ln1_b),
                  full(w_pq), full(keys)],
        out_specs=[tok(D), tok(D), tok(npair), tok(npair)],
        out_shape=[jax.ShapeDtypeStruct((B, T, D), F32),
                   jax.ShapeDtypeStruct((B, T, D), F32),
                   jax.ShapeDtypeStruct((B, T, npair), jnp.int32),
                   jax.ShapeDtypeStruct((B, T, npair), F32)],
        compiler_params=pltpu.CompilerParams(dimension_semantics=("parallel", "parallel"),
                                             vmem_limit_bytes=VMEM_LIMIT),
        name="post",
    )(x, ymix, g1, sh2, sc2, w_out, ln1_g, ln1_b, w_pq, keys)


def _unpack_pair(words):
    lo = pltpu.bitcast(words << 16, F32)
    hi = pltpu.bitcast(words & jnp.int32(-65536), F32)
    return lo, hi


def _gelu_exact(x):
    return 0.5 * x * (1.0 + lax.erf(x * (1.0 / math.sqrt(2.0))))


def _peer_u_kernel(e_ref, x_ref, g_ref, tbl_ref, w_ref, slot_a, slot_b, act_ref, *, tile):
    npair = PEER_HEADS * PEER_TOPK
    lane = lax.broadcasted_iota(jnp.int32, (npair, LANES), 1)

    def gather(t, slot):
        x8 = x_ref[pl.ds(pl.multiple_of(t * SUBLANES, SUBLANES), SUBLANES), :]
        xlo = x8[0:ROW_SUB]
        xhi = x8[ROW_SUB:2 * ROW_SUB]
        for p in range(npair):
            r0 = pl.multiple_of(e_ref[t, p], ROW_SUB)
            lo, hi = _unpack_pair(tbl_ref[pl.ds(r0, ROW_SUB), :])
            slot[pl.ds(p, ROW_SUB, stride=SLOT_STRIDE), :] = lo * xlo + hi * xhi

    def reduce(t, slot):
        part = slot[pl.ds(0, npair), :]
        for j in range(1, ROW_SUB):
            part = part + slot[pl.ds(j * SLOT_STRIDE, npair), :]
        act = jnp.sum(part, axis=1, keepdims=True)
        act_ref[...] = jnp.where(lane == t, act, act_ref[...])

    act_ref[...] = jnp.zeros_like(act_ref)
    gather(0, slot_a)

    def two_tokens(i, carry):
        t0 = 2 * i
        gather(t0 + 1, slot_b)
        reduce(t0, slot_a)
        gather(jnp.minimum(t0 + 2, tile - 1), slot_a)
        reduce(t0 + 1, slot_b)
        return carry

    lax.fori_loop(0, tile // 2, two_tokens, 0)
    w_ref[...] = g_ref[...] * _gelu_exact(act_ref[...].T[0:tile, :])


def _peer_u(e2, x8, gw, table, tile):
    n, npair = gw.shape
    slot_rows = (ROW_SUB - 1) * SLOT_STRIDE + npair
    slot_rows = -(-slot_rows // SUBLANES) * SUBLANES
    tok = pl.BlockSpec((tile, npair), lambda i: (i, 0))
    return pl.pallas_call(
        functools.partial(_peer_u_kernel, tile=tile),
        grid=(n // tile,),
        in_specs=[pl.BlockSpec((tile, npair), lambda i: (i, 0), memory_space=pltpu.SMEM),
                  pl.BlockSpec((tile * SUBLANES, LANES), lambda i: (i, 0)),
                  tok,
                  pl.BlockSpec(memory_space=pltpu.VMEM)],
        out_specs=tok,
        out_shape=jax.ShapeDtypeStruct((n, npair), F32),
        scratch_shapes=[pltpu.VMEM((slot_rows, LANES), F32),
                        pltpu.VMEM((slot_rows, LANES), F32),
                        pltpu.VMEM((npair, LANES), F32)],
        compiler_params=pltpu.CompilerParams(dimension_semantics=("arbitrary",),
                                             vmem_limit_bytes=VMEM_LIMIT),
        name="peer_u",
    )(e2, x8, gw, table)


def _peer_v_kernel(e_ref, w_ref, tbl_ref, o_ref, wb_a, wb_b, *, tile):
    npair = PEER_HEADS * PEER_TOPK
    nacc = 4

    def spread(t, wb):
        row = w_ref[pl.ds(t, 1), :]
        wb[...] = jnp.broadcast_to(row, (LANES, npair)).T

    def accumulate(t, wb):
        acc_lo = [jnp.zeros((ROW_SUB, LANES), F32) for _ in range(nacc)]
        acc_hi = [jnp.zeros((ROW_SUB, LANES), F32) for _ in range(nacc)]
        for p in range(npair):
            r0 = pl.multiple_of(e_ref[t, p], ROW_SUB)
            w = wb[pl.ds(p, 1), :]
            lo, hi = _unpack_pair(tbl_ref[pl.ds(r0, ROW_SUB), :])
            acc_lo[p % nacc] = acc_lo[p % nacc] + w * lo
            acc_hi[p % nacc] = acc_hi[p % nacc] + w * hi
        lo = (acc_lo[0] + acc_lo[1]) + (acc_lo[2] + acc_lo[3])
        hi = (acc_hi[0] + acc_hi[1]) + (acc_hi[2] + acc_hi[3])
        o_ref[pl.ds(pl.multiple_of(t * SUBLANES, SUBLANES), SUBLANES), :] = jnp.concatenate([lo, hi], axis=0)

    spread(0, wb_a)

    def two_tokens(i, carry):
        t0 = 2 * i
        spread(t0 + 1, wb_b)
        accumulate(t0, wb_a)
        spread(jnp.minimum(t0 + 2, tile - 1), wb_a)
        accumulate(t0 + 1, wb_b)
        return carry

    lax.fori_loop(0, tile // 2, two_tokens, 0)


def _peer_v(e2, w2, table, tile):
    n, npair = e2.shape
    return pl.pallas_call(
        functools.partial(_peer_v_kernel, tile=tile),
        grid=(n // tile,),
        in_specs=[pl.BlockSpec((tile, npair), lambda i: (i, 0), memory_space=pltpu.SMEM),
                  pl.BlockSpec((tile, npair), lambda i: (i, 0)),
                  pl.BlockSpec(memory_space=pltpu.VMEM)],
        scratch_shapes=[pltpu.VMEM((npair, LANES), F32), pltpu.VMEM((npair, LANES), F32)],
        out_specs=pl.BlockSpec((tile * SUBLANES, LANES), lambda i: (i, 0)),
        out_shape=jax.ShapeDtypeStruct((n * SUBLANES, LANES), F32),
        compiler_params=pltpu.CompilerParams(dimension_semantics=("arbitrary",),
                                             vmem_limit_bytes=VMEM_LIMIT),
        name="peer_v",
    )(e2, w2, table)


def _pack_table(t):
    bits = lax.bitcast_convert_type(t.astype(BF16), jnp.uint16).astype(jnp.uint32)
    words = bits[:, :ROW_WORDS] | (bits[:, ROW_WORDS:] << 16)
    return lax.bitcast_convert_type(words, jnp.int32).reshape(t.shape[0] * ROW_SUB, LANES)


def _final_kernel(x_ref, p_ref, g2_ref, lg_ref, lb_ref, o_ref, *, alpha):
    o_ref[0] = _ln(alpha * x_ref[0] + g2_ref[0] * p_ref[0], LN_EPS) * lg_ref[...] + lb_ref[...]


def _final(x1, peer, g2, ln2_g, ln2_b, alpha, tm):
    B, T, D = x1.shape
    tok = pl.BlockSpec((1, tm, D), lambda b, i: (b, i, 0))
    vec = pl.BlockSpec((1, D), lambda b, i: (0, 0))
    return pl.pallas_call(
        functools.partial(_final_kernel, alpha=alpha),
        grid=(B, T // tm),
        in_specs=[tok, tok, pl.BlockSpec((1, 1, D), lambda b, i: (b, 0, 0)), vec, vec],
        out_specs=tok,
        out_shape=jax.ShapeDtypeStruct((B, T, D), F32),
        compiler_params=pltpu.CompilerParams(dimension_semantics=("parallel", "parallel")),
        name="final",
    )(x1, peer, g2, ln2_g, ln2_b)


def _retention_consts(chunk, t_valid):
    lg = jnp.log1p(-(2.0 ** (-5.0 - jnp.arange(N_HEADS, dtype=F32))))
    pos = jnp.arange(chunk, dtype=F32)
    diff = pos[:, None] - pos[None, :]
    dmat = jnp.where(diff >= 0, jnp.exp(lg[:, None, None] * jnp.maximum(diff, 0.0)), 0.0)
    inter = jnp.exp(lg[:, None] * (pos + 1.0))[..., None]
    wend = jnp.where(pos < t_valid, jnp.exp(lg[:, None] * (t_valid - 1.0 - pos)), 0.0)[..., None]
    decay = jnp.exp(lg * t_valid)[:, None, None]
    return dmat, inter, wend, decay


def _rotary_tables(T, pos0):
    half = DH // 2
    inv = ROPE_BASE ** (-jnp.arange(half, dtype=F32) / half)
    ang = (pos0 + jnp.arange(T, dtype=F32))[:, None] * inv[None, :]
    cos, sin = jnp.cos(ang), jnp.sin(ang)
    return jnp.concatenate([cos, cos], axis=-1), jnp.concatenate([-sin, sin], axis=-1)


def _layer(x, c, pos0, t_valid, C0, n0, m0, S0, p, alpha):
    B, T, D = x.shape
    W = N_HEADS * DH
    mod = _ada(c, p["w_ada"], p["b_ada"])
    sh1, sc1, g1, sh2, sc2, g2 = [m[:, None, :] for m in jnp.split(mod, 6, axis=-1)]

    w_in, b_in = p["w_in"], p["b_in"]
    g_off = 4 * W
    r_off = g_off + 2 * N_HEADS
    wm = w_in[:, :g_off].astype(BF16)
    wr = w_in[:, r_off:].astype(BF16)
    wg = jnp.pad(w_in[:, g_off:r_off], ((0, 0), (0, LANES - 2 * N_HEADS)))
    bg = jnp.pad(b_in[g_off:r_off], (0, LANES - 2 * N_HEADS))[None, :]
    wgt = w_in[:, g_off:r_off].T
    bgt = b_in[g_off:r_off][:, None]
    tm = min(T, 512)
    zm, zr, zg, zgt = _inproj(x, sh1, sc1, wm, b_in[None, :g_off], wr, b_in[None, r_off:],
                              wg, bg, wgt, bgt, tm)

    chunk = min(T, MIX_CHUNK)
    tv = min(t_valid, chunk) if T == chunk else chunk
    cos2, sin2 = _rotary_tables(T, pos0)
    dmat, inter, wend, decay = _retention_consts(chunk, tv)
    bc = lambda a: jnp.broadcast_to(a[..., None, :], a.shape[:-1] + (SUBLANES, a.shape[-1]))
    n0b = bc(n0.astype(F32))
    m0b = jnp.broadcast_to(m0.astype(F32)[..., None, None], m0.shape + (SUBLANES, DH))
    ymix, C1, n1, m1, S1 = _mixers(zm, zr, zg, zgt, cos2, sin2, dmat, inter, wend, decay,
                                   p["head_g"][None, :], C0.astype(F32), n0b, m0b, S0.astype(F32),
                                   chunk, tv)
    n1 = n1[:, :, 0, :]
    m1 = m1[:, :, 0, 0]

    x1, h2, eidx, gw = _post(x, ymix, g1, sh2, sc2, p["w_out"].astype(BF16), p["ln1_g"][None, :],
                             p["ln1_b"][None, :], p["w_pq"].astype(BF16), p["sub_keys"], alpha,
                             POST_TILE)
    n = B * T
    npair = PEER_HEADS * PEER_TOPK
    tile = min(n, PEER_TILE)
    e2 = eidx.reshape(n, npair)
    wts = _peer_u(e2, h2.reshape(n * SUBLANES, LANES), gw.reshape(n, npair),
                  _pack_table(p["expert_u"]), tile)
    peer = _peer_v(e2, wts, _pack_table(p["expert_v"]), tile)
    y = _final(x1, peer.reshape(B, T, D), g2, p["ln2_g"][None, :], p["ln2_b"][None, :], alpha,
               min(T, 512))
    return y, (C1, n1, m1, S1)


def kernel(x_prompt, x_sample, c_prompt, c_sample, state_mlstm_C, state_mlstm_n, state_mlstm_m,
           state_ret_S, w_ada, b_ada, w_in, b_in, head_g, w_out, ln1_g, ln1_b, w_pq, sub_keys,
           expert_u, expert_v, ln2_g, ln2_b):
    depth = w_ada.shape[0]
    alpha = (2.0 * depth) ** 0.25
    params = dict(w_ada=w_ada, b_ada=b_ada, w_in=w_in, b_in=b_in, head_g=head_g, w_out=w_out,
                  ln1_g=ln1_g, ln1_b=ln1_b, w_pq=w_pq, sub_keys=sub_keys, expert_u=expert_u,
                  expert_v=expert_v, ln2_g=ln2_g, ln2_b=ln2_b)

    def run(x, c, pos0, C0, n0, m0, S0):
        B, T, D = x.shape
        t_valid = T
        if T % LANES:
            x = jnp.pad(x, ((0, 0), (0, LANES - T % LANES), (0, 0)))
        Cs, ns, ms, Ss = [], [], [], []
        for l in range(depth):
            pl_ = {k: v[l] for k, v in params.items()}
            x, (C1, n1, m1, S1) = _layer(x, c, pos0, t_valid, C0[l], n0[l], m0[l], S0[l], pl_, alpha)
            Cs.append(C1); ns.append(n1); ms.append(m1); Ss.append(S1)
        return x[:, :T], jnp.stack(Cs), jnp.stack(ns), jnp.stack(ms), jnp.stack(Ss)

    B = x_prompt.shape[0]
    H = state_mlstm_C.shape[2]
    zC = jnp.zeros((depth, B, H, DH, DH), F32)
    zn = jnp.zeros((depth, B, H, DH), F32)
    zm = jnp.zeros((depth, B, H), F32)
    zS = jnp.zeros((depth, B, H, DH, DH), F32)
    past_len = 1024
    y_p, C_p, n_p, m_p, S_p = run(x_prompt, c_prompt, 0, zC, zn, zm, zS)
    y_s, C_s, n_s, m_s, S_s = run(x_sample, c_sample, past_len, state_mlstm_C, state_mlstm_n,
                                  state_mlstm_m, state_ret_S)
    return (y_p, y_s, C_p, n_p, m_p, S_p, C_s, n_s, m_s, S_s)
```

```python
import functools
import math

import jax
import jax.numpy as jnp
from jax import lax
from jax.experimental import pallas as pl
from jax.experimental.pallas import tpu as pltpu

F32 = jnp.float32
BF16 = jnp.bfloat16
HIGHEST = lax.Precision.HIGHEST

LN_EPS = 1e-5
HEAD_EPS = 1e-6
ROPE_BASE = 10000.0
N_HEADS = 4
DH = 128
LANES = 128
SUBLANES = 8
PEER_HEADS = 8
PEER_TOPK = 16
MIX_CHUNK = 256
POST_TILE = 128
PEER_TILE = 128
ROW_WORDS = 512
ROW_SUB = ROW_WORDS // LANES
SLOT_STRIDE = 65
MASKED_GATE = -1e30
VMEM_LIMIT = 56 * 1024 * 1024


def _ln(x, eps):
    mu = jnp.mean(x, axis=-1, keepdims=True)
    xc = x - mu
    var = jnp.mean(xc * xc, axis=-1, keepdims=True)
    return xc * lax.rsqrt(var + eps)


def _dot(a, b):
    return jnp.dot(a.astype(BF16), b.astype(BF16), preferred_element_type=F32)


def _dot_nt(a, b):
    return lax.dot_general(a.astype(BF16), b.astype(BF16), (((1,), (1,)), ((), ())),
                           preferred_element_type=F32)


def _dot_tn(a, b):
    return lax.dot_general(a.astype(BF16), b.astype(BF16), (((0,), (0,)), ((), ())),
                           preferred_element_type=F32)


def _log_sigmoid(x):
    return jnp.minimum(x, 0.0) - jnp.log1p(jnp.exp(-jnp.abs(x)))


def _ada_kernel(c_ref, w_ref, b_ref, o_ref):
    c = c_ref[...]
    s = c * jax.nn.sigmoid(c)
    o_ref[...] = jnp.dot(s, w_ref[...], preferred_element_type=F32, precision=HIGHEST) + b_ref[...]


def _ada(c, w_ada, b_ada):
    B, D = c.shape
    n = w_ada.shape[1]
    return pl.pallas_call(
        _ada_kernel,
        grid=(n // D,),
        in_specs=[pl.BlockSpec((B, D), lambda j: (0, 0)),
                  pl.BlockSpec((D, D), lambda j: (0, j)),
                  pl.BlockSpec((1, D), lambda j: (0, j))],
        out_specs=pl.BlockSpec((B, D), lambda j: (0, j)),
        out_shape=jax.ShapeDtypeStruct((B, n), F32),
        name="ada",
    )(c, w_ada, b_ada.reshape(1, n))


def _inproj_kernel(x_ref, sh_ref, sc_ref, wm_ref, bm_ref, wr_ref, br_ref, wg_ref, bg_ref,
                   wgt_ref, bgt_ref, zm_ref, zr_ref, zg_ref, zgt_ref):
    h = _ln(x_ref[0], LN_EPS) * (1.0 + sc_ref[0]) + sh_ref[0]
    hb = h.astype(BF16)
    zm_ref[0] = jnp.dot(hb, wm_ref[...], preferred_element_type=F32) + bm_ref[...]
    zr_ref[0] = jnp.dot(hb, wr_ref[...], preferred_element_type=F32) + br_ref[...]
    zg_ref[0] = jnp.dot(h, wg_ref[...], preferred_element_type=F32, precision=HIGHEST) + bg_ref[...]
    zgt_ref[0] = lax.dot_general(wgt_ref[...], h, (((1,), (1,)), ((), ())),
                                 preferred_element_type=F32, precision=HIGHEST) + bgt_ref[...]


def _inproj(x, sh, sc, wm, bm, wr, br, wg, bg, wgt, bgt, tm):
    B, T, D = x.shape
    wm_n, wr_n = wm.shape[1], wr.shape[1]
    full = lambda a: pl.BlockSpec(a.shape, lambda b, i: (0,) * a.ndim)
    return pl.pallas_call(
        _inproj_kernel,
        grid=(B, T // tm),
        in_specs=[pl.BlockSpec((1, tm, D), lambda b, i: (b, i, 0)),
                  pl.BlockSpec((1, 1, D), lambda b, i: (b, 0, 0)),
                  pl.BlockSpec((1, 1, D), lambda b, i: (b, 0, 0)),
                  full(wm), full(bm), full(wr), full(br), full(wg), full(bg), full(wgt), full(bgt)],
        out_specs=[pl.BlockSpec((1, tm, wm_n), lambda b, i: (b, i, 0)),
                   pl.BlockSpec((1, tm, wr_n), lambda b, i: (b, i, 0)),
                   pl.BlockSpec((1, tm, LANES), lambda b, i: (b, i, 0)),
                   pl.BlockSpec((1, SUBLANES, tm), lambda b, i: (b, 0, i))],
        out_shape=[jax.ShapeDtypeStruct((B, T, wm_n), F32),
                   jax.ShapeDtypeStruct((B, T, wr_n), F32),
                   jax.ShapeDtypeStruct((B, T, LANES), F32),
                   jax.ShapeDtypeStruct((B, SUBLANES, T), F32)],
        compiler_params=pltpu.CompilerParams(dimension_semantics=("parallel", "parallel"),
                                             vmem_limit_bytes=VMEM_LIMIT),
        name="inproj",
    )(x, sh, sc, wm, bm, wr, br, wg, bg, wgt, bgt)


def _head_norm(t):
    return _ln(t, HEAD_EPS)


def _mixers_kernel(zm_ref, zr_ref, zg_ref, zgt_ref, cos_ref, sin_ref, dmat_ref, inter_ref,
                   wend_ref, decay_ref, hg_ref, c0_ref, n0_ref, m0_ref, s0_ref,
                   y_ref, c_ref, n_ref, m_ref, s_ref, *, chunk, t_valid):
    L = chunk
    W = N_HEADS * DH

    @pl.when(pl.program_id(1) == 0)
    def _():
        c_ref[...] = c0_ref[...]
        n_ref[...] = n0_ref[...]
        m_ref[...] = m0_ref[...]
        s_ref[...] = s0_ref[...]

    row = lax.broadcasted_iota(jnp.int32, (L, L), 0)
    col = lax.broadcasted_iota(jnp.int32, (L, L), 1)
    tri = col <= row
    zg = zg_ref[0]
    zgt = zgt_ref[0]
    hg = hg_ref[...]
    if t_valid < L:
        valid_col = lax.broadcasted_iota(jnp.int32, (L, 1), 0) < t_valid
        valid_row = lax.broadcasted_iota(jnp.int32, (1, L), 1) < t_valid
    scale = DH ** -0.5

    for h in range(N_HEADS):
        lo, hi = h * DH, (h + 1) * DH
        q = zm_ref[0, :, lo:hi]
        k = zm_ref[0, :, W + lo:W + hi] * scale
        v = zm_ref[0, :, 2 * W + lo:2 * W + hi]
        og = zm_ref[0, :, 3 * W + lo:3 * W + hi]
        ig_col = zg[:, h:h + 1]
        lf_col = _log_sigmoid(zg[:, N_HEADS + h:N_HEADS + h + 1])
        ig_row = zgt[h:h + 1, :]
        lf_row = _log_sigmoid(zgt[N_HEADS + h:N_HEADS + h + 1, :])
        if t_valid < L:
            ig_col = jnp.where(valid_col, ig_col, MASKED_GATE)
            ig_row = jnp.where(valid_row, ig_row, MASKED_GATE)
            lf_col = jnp.where(valid_col, lf_col, 0.0)
            lf_row = jnp.where(valid_row, lf_row, 0.0)
        C = c_ref[0, h]
        n_row = n_ref[0, h, 0:1, :]
        m_prev = m_ref[0, h, 0:1, 0:1]

        b_col = jnp.sum(jnp.where(tri, lf_row, 0.0), axis=1, keepdims=True)
        b_row = jnp.sum(jnp.where(row <= col, lf_col, 0.0), axis=0, keepdims=True)
        a_col = ig_col - b_col
        a_row = ig_row - b_row
        cm_col = jnp.max(jnp.where(tri, a_row, -jnp.inf), axis=1, keepdims=True)
        M_col = jnp.maximum(m_prev, cm_col)
        dm = jnp.exp(jnp.where(tri, a_row - M_col, -jnp.inf))
        inter = jnp.exp(m_prev - M_col)
        s = _dot_nt(q, k) * dm
        num = inter * _dot(q, C) + _dot(s, v)
        den = inter * jnp.sum(q * n_row, axis=1, keepdims=True) + jnp.sum(s, axis=1, keepdims=True)
        hh = num / jnp.maximum(jnp.abs(den), jnp.exp(-(b_col + M_col)))
        M_end = M_col[L - 1:L, :]
        w_end = jnp.exp(a_col - M_end)
        decay = jnp.exp(m_prev - M_end)
        kw = k * w_end
        c_ref[0, h] = decay * C + _dot_tn(kw, v)
        n_new = decay * n_row + jnp.sum(kw, axis=0, keepdims=True)
        n_ref[0, h] = jnp.broadcast_to(n_new, (SUBLANES, DH))
        m_new = b_col[L - 1:L, :] + M_end
        m_ref[0, h] = jnp.broadcast_to(m_new, (SUBLANES, DH))
        out_a = _head_norm(hh) * jax.nn.sigmoid(og)
        y_ref[0, :, lo:hi] = out_a * hg[:, lo:hi]

        cos2 = cos_ref[...]
        sin2 = sin_ref[...]
        rq = zr_ref[0, :, lo:hi]
        rk = zr_ref[0, :, W + lo:W + hi]
        rv = zr_ref[0, :, 2 * W + lo:2 * W + hi]
        rg = zr_ref[0, :, 3 * W + lo:3 * W + hi]
        qb = rq * cos2 + pltpu.roll(rq, DH // 2, axis=1) * sin2
        kb = (rk * cos2 + pltpu.roll(rk, DH // 2, axis=1) * sin2) * scale
        S = s_ref[0, h]
        sr = _dot_nt(qb, kb) * dmat_ref[h]
        o = _dot(sr, rv) + inter_ref[h] * _dot(qb, S)
        s_ref[0, h] = decay_ref[h] * S + _dot_tn(kb * wend_ref[h], rv)
        out_b = _head_norm(o) * (rg * jax.nn.sigmoid(rg))
        y_ref[0, :, W + lo:W + hi] = out_b * hg[:, W + lo:W + hi]


def _mixers(zm, zr, zg, zgt, cos2, sin2, dmat, inter, wend, decay, head_g, C0, n0, m0, S0,
            chunk, t_valid):
    B, T, _ = zm.shape
    nC = T // chunk
    D = 2 * N_HEADS * DH
    full = lambda a: pl.BlockSpec(a.shape, lambda b, c: (0,) * a.ndim)
    st4 = pl.BlockSpec((1, N_HEADS, DH, DH), lambda b, c: (b, 0, 0, 0))
    st3 = pl.BlockSpec((1, N_HEADS, SUBLANES, DH), lambda b, c: (b, 0, 0, 0))
    return pl.pallas_call(
        functools.partial(_mixers_kernel, chunk=chunk, t_valid=t_valid),
        grid=(B, nC),
        in_specs=[pl.BlockSpec((1, chunk, zm.shape[2]), lambda b, c: (b, c, 0)),
                  pl.BlockSpec((1, chunk, zr.shape[2]), lambda b, c: (b, c, 0)),
                  pl.BlockSpec((1, chunk, LANES), lambda b, c: (b, c, 0)),
                  pl.BlockSpec((1, SUBLANES, chunk), lambda b, c: (b, 0, c)),
                  pl.BlockSpec((chunk, DH), lambda b, c: (c, 0)),
                  pl.BlockSpec((chunk, DH), lambda b, c: (c, 0)),
                  full(dmat), full(inter), full(wend), full(decay), full(head_g),
                  st4, st3, st3, st4],
        out_specs=[pl.BlockSpec((1, chunk, D), lambda b, c: (b, c, 0)), st4, st3, st3, st4],
        out_shape=[jax.ShapeDtypeStruct((B, T, D), F32),
                   jax.ShapeDtypeStruct(C0.shape, F32),
                   jax.ShapeDtypeStruct(n0.shape, F32),
                   jax.ShapeDtypeStruct(m0.shape, F32),
                   jax.ShapeDtypeStruct(S0.shape, F32)],
        compiler_params=pltpu.CompilerParams(dimension_semantics=("parallel", "arbitrary"),
                                             vmem_limit_bytes=VMEM_LIMIT),
        name="mixers",
    )(zm, zr, zg, zgt, cos2, sin2, dmat, inter, wend, decay, head_g, C0, n0, m0, S0)


def _argmax_blocks(vals, tags, extras=()):
    v, t = list(vals), list(tags)
    ex = [list(e) for e in extras]
    while len(v) > 1:
        nv, nt, ne = [], [], [[] for _ in ex]
        for a in range(0, len(v) - 1, 2):
            take = v[a] >= v[a + 1]
            nv.append(jnp.where(take, v[a], v[a + 1]))
            nt.append(jnp.where(take, t[a], t[a + 1]))
            for q, e in enumerate(ex):
                ne[q].append(jnp.where(take, e[a], e[a + 1]))
        if len(v) % 2:
            nv.append(v[-1])
            nt.append(t[-1])
            for q, e in enumerate(ex):
                ne[q].append(e[-1])
        v, t, ex = nv, nt, ne
    m = jnp.max(v[0], axis=0, keepdims=True)
    tag = jnp.min(jnp.where(v[0] == m, t[0], jnp.int32(2 ** 30)), axis=0, keepdims=True)
    outs = [jnp.sum(jnp.where(t[0] == tag, e[0], 0), axis=0, keepdims=True) for e in ex]
    return m, tag, outs


def _topk_sorted(s, k):
    n, tok = s.shape
    nb = n // SUBLANES
    sub = lax.broadcasted_iota(jnp.int32, (SUBLANES, tok), 0)
    blocks = [s[b * SUBLANES:(b + 1) * SUBLANES] for b in range(nb)]
    tags = [sub + b * SUBLANES for b in range(nb)]
    v_rows, i_rows = [], []
    v_blk = [jnp.zeros((SUBLANES, tok), F32) for _ in range(k // SUBLANES)]
    i_blk = [jnp.zeros((SUBLANES, tok), jnp.int32) for _ in range(k // SUBLANES)]
    for r in range(k):
        m, i, _ = _argmax_blocks(blocks, tags)
        v_rows.append(m)
        i_rows.append(i)
        at = sub == (r % SUBLANES)
        v_blk[r // SUBLANES] = jnp.where(at, m, v_blk[r // SUBLANES])
        i_blk[r // SUBLANES] = jnp.where(at, i, i_blk[r // SUBLANES])
        blocks = [jnp.where(t == i, -jnp.inf, b) for b, t in zip(blocks, tags)]
    return v_rows, i_rows, v_blk, i_blk


def _product_topk(top0, top1, nk):
    K = PEER_TOPK
    v0_rows, i0_rows, v0_blk, i0_blk = top0
    v1_rows, i1_rows, v1_blk, i1_blk = top1
    tok = v0_rows[0].shape[1]
    sub = lax.broadcasted_iota(jnp.int32, (SUBLANES, tok), 0)
    vals, flats, experts = [], [], []
    for i in range(SUBLANES):
        for jb in range(K // SUBLANES):
            limit = K // (i + 1) - jb * SUBLANES
            if limit <= 0:
                continue
            blk = v0_rows[i] + v1_blk[jb]
            if limit < SUBLANES:
                blk = jnp.where(sub < limit, blk, -jnp.inf)
            vals.append(blk)
            flats.append(sub + (i * K + jb * SUBLANES))
            experts.append(i0_rows[i] * nk + i1_blk[jb])
    vals.append(v0_blk[1] + v1_rows[0])
    flats.append((sub + SUBLANES) * K)
    experts.append(i0_blk[1] * nk + i1_rows[0])
    tv, te = [], []
    for _ in range(K):
        m, f, (e,) = _argmax_blocks(vals, flats, extras=(experts,))
        tv.append(m)
        te.append(e)
        vals = [jnp.where(t == f, -jnp.inf, b) for b, t in zip(vals, flats)]
    return tv, te


def _post_kernel(x_ref, y_ref, g1_ref, sh2_ref, sc2_ref, wout_ref, l1g_ref, l1b_ref, wpq_ref,
                 keys_ref, x1_ref, h2_ref, e_ref, g_ref, *, alpha):
    mix = _dot(y_ref[0], wout_ref[...])
    x1 = _ln(alpha * x_ref[0] + g1_ref[0] * mix, LN_EPS) * l1g_ref[...] + l1b_ref[...]
    x1_ref[0] = x1
    h2 = _ln(x1, LN_EPS) * (1.0 + sc2_ref[0]) + sh2_ref[0]
    h2_ref[0] = h2
    qall = _dot(h2, wpq_ref[...])
    nk = keys_ref.shape[1]
    K = PEER_TOPK
    e_rows, g_rows = [], []
    for h in range(PEER_HEADS):
        tops = []
        for c in range(2):
            lo = (h * 2 + c) * DH
            st = _dot_nt(keys_ref[c], qall[:, lo:lo + DH])
            tops.append(_topk_sorted(st, K))
        tv, te = _product_topk(tops[0], tops[1], nk)
        tv = jnp.concatenate(tv, axis=0)
        ex = jnp.exp(tv - tv[0:1, :])
        g_rows.append(ex / jnp.sum(ex, axis=0, keepdims=True))
        e_rows.append(jnp.concatenate(te, axis=0))
    e_all = jnp.concatenate(e_rows, axis=0) * ROW_SUB
    g_all = jnp.concatenate(g_rows, axis=0)
    e_ref[0] = pltpu.bitcast(pltpu.bitcast(e_all, F32).T, jnp.int32)
    g_ref[0] = g_all.T


def _post(x, ymix, g1, sh2, sc2, w_out, ln1_g, ln1_b, w_pq, keys, alpha, tm):
    B, T, D = x.shape
    npair = PEER_HEADS * PEER_TOPK
    full = lambda a: pl.BlockSpec(a.shape, lambda b, i: (0,) * a.ndim)
    tok = lambda w: pl.BlockSpec((1, tm, w), lambda b, i: (b, i, 0))
    bat = pl.BlockSpec((1, 1, D), lambda b, i: (b, 0, 0))
    return pl.pallas_call(
        functools.partial(_post_kernel, alpha=alpha),
        grid=(B, T // tm),
        in_specs=[tok(D), tok(D), bat, bat, bat, full(w_out), full(ln1_g), full(ln1_b),
                  full(w_pq), full(keys)],
        out_specs=[tok(D), tok(D), tok(npair), tok(npair)],
        out_shape=[jax.ShapeDtypeStruct((B, T, D), F32),
                   jax.ShapeDtypeStruct((B, T, D), F32),
                   jax.ShapeDtypeStruct((B, T, npair), jnp.int32),
                   jax.ShapeDtypeStruct((B, T, npair), F32)],
        compiler_params=pltpu.CompilerParams(dimension_semantics=("parallel", "parallel"),
                                             vmem_limit_bytes=VMEM_LIMIT),
        name="post",
    )(x, ymix, g1, sh2, sc2, w_out, ln1_g, ln1_b, w_pq, keys)


def _unpack_pair(words):
    lo = pltpu.bitcast(words << 16, F32)
    hi = pltpu.bitcast(words & jnp.int32(-65536), F32)
    return lo, hi


def _gelu_exact(x):
    return 0.5 * x * (1.0 + lax.erf(x * (1.0 / math.sqrt(2.0))))


def _index_copy(e_hbm, e_scrs, sems, step, s):
    words = e_scrs[s].shape[0]
    return pltpu.make_async_copy(e_hbm.at[pl.ds((step * 2 + s) * words, words)], e_scrs[s], sems.at[s])


def _with_prefetched_indices(e_hbm, e_scrs, sems, process):
    step = pl.program_id(0)

    @pl.when(step == 0)
    def _():
        _index_copy(e_hbm, e_scrs, sems, step, 0).start()

    _index_copy(e_hbm, e_scrs, sems, step, 1).start()
    _index_copy(e_hbm, e_scrs, sems, step, 0).wait()
    process(0, e_scrs[0])
    _index_copy(e_hbm, e_scrs, sems, step, 1).wait()

    @pl.when(step + 1 < pl.num_programs(0))
    def _():
        _index_copy(e_hbm, e_scrs, sems, step + 1, 0).start()

    process(1, e_scrs[1])


def _gather_pair_words(tbl_ref, e_idx, k, q, half, sub):
    ra = pl.multiple_of(e_idx.at[pl.ds(q * sub, sub)][k], ROW_SUB)
    rb = pl.multiple_of(e_idx.at[pl.ds((half + q) * sub, sub)][k], ROW_SUB)
    return jnp.concatenate([tbl_ref[pl.ds(ra, ROW_SUB), :], tbl_ref[pl.ds(rb, ROW_SUB), :]], axis=0)


def _peer_u_kernel(e_hbm, x_ref, g_ref, tbl_ref, w_ref, slot_a, slot_b, act_ref, e_scr0, e_scr1, sems,
                   *, tile):
    npair = PEER_HEADS * PEER_TOPK
    half = npair // 2
    sub = tile // 2
    lane = lax.broadcasted_iota(jnp.int32, (npair, LANES), 1)

    def process(s, e_idx):
        first = s * sub

        def gather(k, slot):
            t = first + k
            x8 = x_ref[pl.ds(pl.multiple_of(t * SUBLANES, SUBLANES), SUBLANES), :]
            xlo = jnp.concatenate([x8[0:ROW_SUB], x8[0:ROW_SUB]], axis=0)
            xhi = jnp.concatenate([x8[ROW_SUB:], x8[ROW_SUB:]], axis=0)
            for q in range(half):
                lo, hi = _unpack_pair(_gather_pair_words(tbl_ref, e_idx, k, q, half, sub))
                slot[pl.ds(q, SUBLANES, stride=SLOT_STRIDE), :] = lo * xlo + hi * xhi

        def reduce(k, slot):
            parts = []
            for g in range(2):
                part = slot[pl.ds(g * ROW_SUB * SLOT_STRIDE, half), :]
                for j in range(1, ROW_SUB):
                    part = part + slot[pl.ds((g * ROW_SUB + j) * SLOT_STRIDE, half), :]
                parts.append(part)
            part = jnp.concatenate(parts, axis=0)
            act = jnp.sum(part, axis=1, keepdims=True)
            act_ref[...] = jnp.where(lane == first + k, act, act_ref[...])

        gather(0, slot_a)

        def two_tokens(i, carry):
            k0 = 2 * i
            gather(k0 + 1, slot_b)
            reduce(k0, slot_a)
            gather(jnp.minimum(k0 + 2, sub - 1), slot_a)
            reduce(k0 + 1, slot_b)
            return carry

        lax.fori_loop(0, sub // 2, two_tokens, 0)

    act_ref[...] = jnp.zeros_like(act_ref)
    _with_prefetched_indices(e_hbm, (e_scr0, e_scr1), sems, process)
    w_ref[...] = g_ref[...] * _gelu_exact(act_ref[...].T)


def _peer_u(eflat, x8, gw, table, tile):
    n, npair = gw.shape
    slot_rows = (SUBLANES - 1) * SLOT_STRIDE + npair // 2
    slot_rows = -(-slot_rows // SUBLANES) * SUBLANES
    tok = pl.BlockSpec((tile, npair), lambda i: (i, 0))
    return pl.pallas_call(
        functools.partial(_peer_u_kernel, tile=tile),
        grid=(n // tile,),
        in_specs=[pl.BlockSpec(memory_space=pl.ANY),
                  pl.BlockSpec((tile * SUBLANES, LANES), lambda i: (i, 0)),
                  tok,
                  pl.BlockSpec(memory_space=pltpu.VMEM)],
        out_specs=tok,
        out_shape=jax.ShapeDtypeStruct((n, npair), F32),
        scratch_shapes=[pltpu.VMEM((slot_rows, LANES), F32),
                        pltpu.VMEM((slot_rows, LANES), F32),
                        pltpu.VMEM((npair, LANES), F32),
                        pltpu.SMEM((tile // 2 * npair,), jnp.int32),
                        pltpu.SMEM((tile // 2 * npair,), jnp.int32),
                        pltpu.SemaphoreType.DMA((2,))],
        compiler_params=pltpu.CompilerParams(dimension_semantics=("arbitrary",),
                                             vmem_limit_bytes=VMEM_LIMIT),
        name="peer_u",
    )(eflat, x8, gw, table)


def _peer_v_kernel(e_hbm, w_ref, tbl_ref, o_ref, wb_a, wb_b, e_scr0, e_scr1, sems, *, tile):
    npair = PEER_HEADS * PEER_TOPK
    half = npair // 2
    sub = tile // 2
    nacc = 4
    top = lax.broadcasted_iota(jnp.int32, (SUBLANES, LANES), 0) < ROW_SUB

    def process(s, e_idx):
        first = s * sub

        def spread(k, wb):
            row = w_ref[pl.ds(first + k, 1), :]
            wb[...] = jnp.broadcast_to(row, (LANES, npair)).T

        def accumulate(k, wb):
            acc_lo = [jnp.zeros((SUBLANES, LANES), F32) for _ in range(nacc)]
            acc_hi = [jnp.zeros((SUBLANES, LANES), F32) for _ in range(nacc)]
            for q in range(half):
                lo, hi = _unpack_pair(_gather_pair_words(tbl_ref, e_idx, k, q, half, sub))
                w =jnp.where(top, wb[pl.ds(q, 1), :], wb[pl.ds(half + q, 1), :])
                acc_lo[q % nacc] = acc_lo[q % nacc] + w * lo
                acc_hi[q % nacc] = acc_hi[q % nacc] + w * hi
            lo = (acc_lo[0] + acc_lo[1]) + (acc_lo[2] + acc_lo[3])
            hi = (acc_hi[0] + acc_hi[1]) + (acc_hi[2] + acc_hi[3])
            lo = lo[0:ROW_SUB] + lo[ROW_SUB:]
            hi = hi[0:ROW_SUB] + hi[ROW_SUB:]
            rows = pl.ds(pl.multiple_of((first + k) * SUBLANES, SUBLANES), SUBLANES)
            o_ref[rows, :] = jnp.concatenate([lo, hi], axis=0)

        spread(0, wb_a)

        def two_tokens(i, carry):
            k0 = 2 * i
            spread(k0 + 1, wb_b)
            accumulate(k0, wb_a)
            spread(jnp.minimum(k0 + 2, sub - 1), wb_a)
            accumulate(k0 + 1, wb_b)
            return carry

        lax.fori_loop(0, sub // 2, two_tokens, 0)

    _with_prefetched_indices(e_hbm, (e_scr0, e_scr1), sems, process)


def _peer_v(eflat, w2, table, tile):
    n, npair = w2.shape
    return pl.pallas_call(
        functools.partial(_peer_v_kernel, tile=tile),
        grid=(n // tile,),
        in_specs=[pl.BlockSpec(memory_space=pl.ANY),
                  pl.BlockSpec((tile, npair), lambda i: (i, 0)),
                  pl.BlockSpec(memory_space=pltpu.VMEM)],
        out_specs=pl.BlockSpec((tile * SUBLANES, LANES), lambda i: (i, 0)),
        out_shape=jax.ShapeDtypeStruct((n * SUBLANES, LANES), F32),
        scratch_shapes=[pltpu.VMEM((npair, LANES), F32),
                        pltpu.VMEM((npair, LANES), F32),
                        pltpu.SMEM((tile // 2 * npair,), jnp.int32),
                        pltpu.SMEM((tile // 2 * npair,), jnp.int32),
                        pltpu.SemaphoreType.DMA((2,))],
        compiler_params=pltpu.CompilerParams(dimension_semantics=("arbitrary",),
                                             vmem_limit_bytes=VMEM_LIMIT),
        name="peer_v",
    )(eflat, w2, table)


def _pack_table(t):
    bits = lax.bitcast_convert_type(t.astype(BF16), jnp.uint16).astype(jnp.uint32)
    words = bits[:, :ROW_WORDS] | (bits[:, ROW_WORDS:] << 16)
    return lax.bitcast_convert_type(words, jnp.int32).reshape(t.shape[0] * ROW_SUB, LANES)


def _final_kernel(x_ref, p_ref, g2_ref, lg_ref, lb_ref, o_ref, *, alpha):
    o_ref[0] = _ln(alpha * x_ref[0] + g2_ref[0] * p_ref[0], LN_EPS) * lg_ref[...] + lb_ref[...]


def _final(x1, peer, g2, ln2_g, ln2_b, alpha, tm):
    B, T, D = x1.shape
    tok = pl.BlockSpec((1, tm, D), lambda b, i: (b, i, 0))
    vec = pl.BlockSpec((1, D), lambda b, i: (0, 0))
    return pl.pallas_call(
        functools.partial(_final_kernel, alpha=alpha),
        grid=(B, T // tm),
        in_specs=[tok, tok, pl.BlockSpec((1, 1, D), lambda b, i: (b, 0, 0)), vec, vec],
        out_specs=tok,
        out_shape=jax.ShapeDtypeStruct((B, T, D), F32),
        compiler_params=pltpu.CompilerParams(dimension_semantics=("parallel", "parallel")),
        name="final",
    )(x1, peer, g2, ln2_g, ln2_b)


def _retention_consts(chunk, t_valid):
    lg = jnp.log1p(-(2.0 ** (-5.0 - jnp.arange(N_HEADS, dtype=F32))))
    pos = jnp.arange(chunk, dtype=F32)
    diff = pos[:, None] - pos[None, :]
    dmat = jnp.where(diff >= 0, jnp.exp(lg[:, None, None] * jnp.maximum(diff, 0.0)), 0.0)
    inter = jnp.exp(lg[:, None] * (pos + 1.0))[..., None]
    wend = jnp.where(pos < t_valid, jnp.exp(lg[:, None] * (t_valid - 1.0 - pos)), 0.0)[..., None]
    decay = jnp.exp(lg * t_valid)[:, None, None]
    return dmat, inter, wend, decay


def _rotary_tables(T, pos0):
    half = DH // 2
    inv = ROPE_BASE ** (-jnp.arange(half, dtype=F32) / half)
    ang = (pos0 + jnp.arange(T, dtype=F32))[:, None] * inv[None, :]
    cos, sin = jnp.cos(ang), jnp.sin(ang)
    return jnp.concatenate([cos, cos], axis=-1), jnp.concatenate([-sin, sin], axis=-1)


def _layer(x, c, pos0, t_valid, C0, n0, m0, S0, p, alpha):
    B, T, D = x.shape
    W = N_HEADS * DH
    mod = _ada(c, p["w_ada"], p["b_ada"])
    sh1, sc1, g1, sh2, sc2, g2 = [m[:, None, :] for m in jnp.split(mod, 6, axis=-1)]

    w_in, b_in = p["w_in"], p["b_in"]
    g_off = 4 * W
    r_off = g_off + 2 * N_HEADS
    wm = w_in[:, :g_off].astype(BF16)
    wr = w_in[:, r_off:].astype(BF16)
    wg = jnp.pad(w_in[:, g_off:r_off], ((0, 0), (0, LANES - 2 * N_HEADS)))
    bg = jnp.pad(b_in[g_off:r_off], (0, LANES - 2 * N_HEADS))[None, :]
    wgt = w_in[:, g_off:r_off].T
    bgt = b_in[g_off:r_off][:, None]
    tm = min(T, 512)
    zm, zr, zg, zgt = _inproj(x, sh1, sc1, wm, b_in[None, :g_off], wr, b_in[None, r_off:],
                              wg, bg, wgt, bgt, tm)

    chunk = min(T, MIX_CHUNK)
    tv = min(t_valid, chunk) if T == chunk else chunk
    cos2, sin2 = _rotary_tables(T, pos0)
    dmat, inter, wend, decay = _retention_consts(chunk, tv)
    bc = lambda a: jnp.broadcast_to(a[..., None, :], a.shape[:-1] + (SUBLANES, a.shape[-1]))
    n0b = bc(n0.astype(F32))
    m0b = jnp.broadcast_to(m0.astype(F32)[..., None, None], m0.shape + (SUBLANES, DH))
    ymix, C1, n1, m1, S1 = _mixers(zm, zr, zg, zgt, cos2, sin2, dmat, inter, wend, decay,
                                   p["head_g"][None, :], C0.astype(F32), n0b, m0b, S0.astype(F32),
                                   chunk, tv)
    n1 = n1[:, :, 0, :]
    m1 = m1[:, :, 0, 0]

    x1, h2, eidx, gw = _post(x, ymix, g1, sh2, sc2, p["w_out"].astype(BF16), p["ln1_g"][None, :],
                             p["ln1_b"][None, :], p["w_pq"].astype(BF16), p["sub_keys"], alpha,
                             POST_TILE)
    n = B * T
    npair = PEER_HEADS * PEER_TOPK
    tile = min(n, PEER_TILE)
    sub = tile // 2
    eflat = eidx.reshape(n // sub, sub, npair).transpose(0, 2, 1).reshape(n * npair)
    wts = _peer_u(eflat, h2.reshape(n * SUBLANES, LANES), gw.reshape(n, npair),
                  _pack_table(p["expert_u"]), tile)
    peer = _peer_v(eflat, wts, _pack_table(p["expert_v"]), tile)
    y = _final(x1, peer.reshape(B, T, D), g2, p["ln2_g"][None, :], p["ln2_b"][None, :], alpha,
               min(T, 512))
    return y, (C1, n1, m1, S1)


def kernel(x_prompt, x_sample, c_prompt, c_sample, state_mlstm_C, state_mlstm_n, state_mlstm_m,
           state_ret_S, w_ada, b_ada, w_in, b_in, head_g, w_out, ln1_g, ln1_b, w_pq, sub_keys,
           expert_u, expert_v, ln2_g, ln2_b):
    depth = w_ada.shape[0]
    alpha = (2.0 * depth) ** 0.25
    params = dict(w_ada=w_ada, b_ada=b_ada, w_in=w_in, b_in=b_in, head_g=head_g, w_out=w_out,
                  ln1_g=ln1_g, ln1_b=ln1_b, w_pq=w_pq, sub_keys=sub_keys, expert_u=expert_u,
                  expert_v=expert_v, ln2_g=ln2_g, ln2_b=ln2_b)

    def run(x, c, pos0, C0, n0, m0, S0):
        B, T, D = x.shape
        t_valid = T
        if T % LANES:
            x = jnp.pad(x, ((0, 0), (0, LANES - T % LANES), (0, 0)))
        Cs, ns, ms, Ss = [], [], [], []
        for l in range(depth):
            pl_ = {k: v[l] for k, v in params.items()}
            x, (C1, n1, m1, S1) = _layer(x, c, pos0, t_valid, C0[l], n0[l], m0[l], S0[l], pl_, alpha)
            Cs.append(C1); ns.append(n1); ms.append(m1); Ss.append(S1)
        return x[:, :T], jnp.stack(Cs), jnp.stack(ns), jnp.stack(ms), jnp.stack(Ss)

    B = x_prompt.shape[0]
    H = state_mlstm_C.shape[2]
    zC = jnp.zeros((depth, B, H, DH, DH), F32)
    zn = jnp.zeros((depth, B, H, DH), F32)
    zm = jnp.zeros((depth, B, H), F32)
    zS = jnp.zeros((depth, B, H, DH, DH), F32)
    past_len = 1024
    y_p, C_p, n_p, m_p, S_p = run(x_prompt, c_prompt, 0, zC, zn, zm, zS)
    y_s, C_s, n_s, m_s, S_s = run(x_sample, c_sample, past_len, state_mlstm_C, state_mlstm_n,
                                  state_mlstm_m, state_ret_S)
    return (y_p, y_s, C_p, n_p, m_p, S_p, C_s, n_s, m_s, S_s)
```

```python
import functools
import math

import jax
import jax.numpy as jnp
from jax import lax
from jax.experimental import pallas as pl
from jax.experimental.pallas import tpu as pltpu

F32 = jnp.float32
BF16 = jnp.bfloat16
HIGHEST = lax.Precision.HIGHEST

LN_EPS = 1e-5
HEAD_EPS = 1e-6
ROPE_BASE = 10000.0
N_HEADS = 4
DH = 128
LANES = 128
SUBLANES = 8
PEER_HEADS = 8
PEER_TOPK = 16
MIX_CHUNK = 256
POST_TILE = 128
PEER_TILE = 128
PEER_UNROLL = 8
ROW_WORDS = 512
ROW_SUB = ROW_WORDS // LANES
SLOT_STRIDE = 65
MASKED_GATE = -1e30
VMEM_LIMIT = 56 * 1024 * 1024


def _ln(x, eps):
    mu = jnp.mean(x, axis=-1, keepdims=True)
    xc = x - mu
    var = jnp.mean(xc * xc, axis=-1, keepdims=True)
    return xc * lax.rsqrt(var + eps)


def _dot(a, b):
    return jnp.dot(a.astype(BF16), b.astype(BF16), preferred_element_type=F32)


def _dot_nt(a, b):
    return lax.dot_general(a.astype(BF16), b.astype(BF16), (((1,), (1,)), ((), ())),
                           preferred_element_type=F32)


def _dot_tn(a, b):
    return lax.dot_general(a.astype(BF16), b.astype(BF16), (((0,), (0,)), ((), ())),
                           preferred_element_type=F32)


def _log_sigmoid(x):
    return jnp.minimum(x, 0.0) - jnp.log1p(jnp.exp(-jnp.abs(x)))


def _ada_kernel(c_ref, w_ref, b_ref, o_ref):
    c = c_ref[...]
    s = c * jax.nn.sigmoid(c)
    o_ref[...] = jnp.dot(s, w_ref[...], preferred_element_type=F32, precision=HIGHEST) + b_ref[...]


def _ada(c, w_ada, b_ada):
    B, D = c.shape
    n = w_ada.shape[1]
    return pl.pallas_call(
        _ada_kernel,
        grid=(n // D,),
        in_specs=[pl.BlockSpec((B, D), lambda j: (0, 0)),
                  pl.BlockSpec((D, D), lambda j: (0, j)),
                  pl.BlockSpec((1, D), lambda j: (0, j))],
        out_specs=pl.BlockSpec((B, D), lambda j: (0, j)),
        out_shape=jax.ShapeDtypeStruct((B, n), F32),
        name="ada",
    )(c, w_ada, b_ada.reshape(1, n))


def _inproj_kernel(x_ref, sh_ref, sc_ref, wm_ref, bm_ref, wr_ref, br_ref, wg_ref, bg_ref,
                   wgt_ref, bgt_ref, zm_ref, zr_ref, zg_ref, zgt_ref):
    h = _ln(x_ref[0], LN_EPS) * (1.0 + sc_ref[0]) + sh_ref[0]
    hb = h.astype(BF16)
    zm_ref[0] = jnp.dot(hb, wm_ref[...], preferred_element_type=F32) + bm_ref[...]
    zr_ref[0] = jnp.dot(hb, wr_ref[...], preferred_element_type=F32) + br_ref[...]
    zg_ref[0] = jnp.dot(h, wg_ref[...], preferred_element_type=F32, precision=HIGHEST) + bg_ref[...]
    zgt_ref[0] = lax.dot_general(wgt_ref[...], h, (((1,), (1,)), ((), ())),
                                 preferred_element_type=F32, precision=HIGHEST) + bgt_ref[...]


def _inproj(x, sh, sc, wm, bm, wr, br, wg, bg, wgt, bgt, tm):
    B, T, D = x.shape
    wm_n, wr_n = wm.shape[1], wr.shape[1]
    full = lambda a: pl.BlockSpec(a.shape, lambda b, i: (0,) * a.ndim)
    return pl.pallas_call(
        _inproj_kernel,
        grid=(B, T // tm),
        in_specs=[pl.BlockSpec((1, tm, D), lambda b, i: (b, i, 0)),
                  pl.BlockSpec((1, 1, D), lambda b, i: (b, 0, 0)),
                  pl.BlockSpec((1, 1, D), lambda b, i: (b, 0, 0)),
                  full(wm), full(bm), full(wr), full(br), full(wg), full(bg), full(wgt), full(bgt)],
        out_specs=[pl.BlockSpec((1, tm, wm_n), lambda b, i: (b, i, 0)),
                   pl.BlockSpec((1, tm, wr_n), lambda b, i: (b, i, 0)),
                   pl.BlockSpec((1, tm, LANES), lambda b, i: (b, i, 0)),
                   pl.BlockSpec((1, SUBLANES, tm), lambda b, i: (b, 0, i))],
        out_shape=[jax.ShapeDtypeStruct((B, T, wm_n), F32),
                   jax.ShapeDtypeStruct((B, T, wr_n), F32),
                   jax.ShapeDtypeStruct((B, T, LANES), F32),
                   jax.ShapeDtypeStruct((B, SUBLANES, T), F32)],
        compiler_params=pltpu.CompilerParams(dimension_semantics=("parallel", "parallel"),
                                             vmem_limit_bytes=VMEM_LIMIT),
        name="inproj",
    )(x, sh, sc, wm, bm, wr, br, wg, bg, wgt, bgt)


def _head_norm(t):
    return _ln(t, HEAD_EPS)


def _mixers_kernel(zm_ref, zr_ref, zg_ref, zgt_ref, cos_ref, sin_ref, dmat_ref, inter_ref,
                   wend_ref, decay_ref, hg_ref, c0_ref, n0_ref, m0_ref, s0_ref,
                   y_ref, c_ref, n_ref, m_ref, s_ref, *, chunk, t_valid):
    L = chunk
    W = N_HEADS * DH

    @pl.when(pl.program_id(1) == 0)
    def _():
        c_ref[...] = c0_ref[...]
        n_ref[...] = n0_ref[...]
        m_ref[...] = m0_ref[...]
        s_ref[...] = s0_ref[...]

    row = lax.broadcasted_iota(jnp.int32, (L, L), 0)
    col = lax.broadcasted_iota(jnp.int32, (L, L), 1)
    tri = col <= row
    zg = zg_ref[0]
    zgt = zgt_ref[0]
    hg = hg_ref[...]
    if t_valid < L:
        valid_col = lax.broadcasted_iota(jnp.int32, (L, 1), 0) < t_valid
        valid_row = lax.broadcasted_iota(jnp.int32, (1, L), 1) < t_valid
    scale = DH ** -0.5

    for h in range(N_HEADS):
        lo, hi = h * DH, (h + 1) * DH
        q = zm_ref[0, :, lo:hi]
        k = zm_ref[0, :, W + lo:W + hi] * scale
        v = zm_ref[0, :, 2 * W + lo:2 * W + hi]
        og = zm_ref[0, :, 3 * W + lo:3 * W + hi]
        ig_col = zg[:, h:h + 1]
        lf_col = _log_sigmoid(zg[:, N_HEADS + h:N_HEADS + h + 1])
        ig_row = zgt[h:h + 1, :]
        lf_row = _log_sigmoid(zgt[N_HEADS + h:N_HEADS + h + 1, :])
        if t_valid < L:
            ig_col = jnp.where(valid_col, ig_col, MASKED_GATE)
            ig_row = jnp.where(valid_row, ig_row, MASKED_GATE)
            lf_col = jnp.where(valid_col, lf_col, 0.0)
            lf_row = jnp.where(valid_row, lf_row, 0.0)
        C = c_ref[0, h]
        n_row = n_ref[0, h, 0:1, :]
        m_prev = m_ref[0, h, 0:1, 0:1]

        b_col = jnp.sum(jnp.where(tri, lf_row, 0.0), axis=1, keepdims=True)
        b_row = jnp.sum(jnp.where(row <= col, lf_col, 0.0), axis=0, keepdims=True)
        a_col = ig_col - b_col
        a_row = ig_row - b_row
        cm_col = jnp.max(jnp.where(tri, a_row, -jnp.inf), axis=1, keepdims=True)
        M_col = jnp.maximum(m_prev, cm_col)
        dm = jnp.exp(jnp.where(tri, a_row - M_col, -jnp.inf))
        inter = jnp.exp(m_prev - M_col)
        s = _dot_nt(q, k) * dm
        num = inter * _dot(q, C) + _dot(s, v)
        den = inter * jnp.sum(q * n_row, axis=1, keepdims=True) + jnp.sum(s, axis=1, keepdims=True)
        hh = num / jnp.maximum(jnp.abs(den), jnp.exp(-(b_col + M_col)))
        M_end = M_col[L - 1:L, :]
        w_end = jnp.exp(a_col - M_end)
        decay = jnp.exp(m_prev - M_end)
        kw = k * w_end
        c_ref[0, h] = decay * C + _dot_tn(kw, v)
        n_new = decay * n_row + jnp.sum(kw, axis=0, keepdims=True)
        n_ref[0, h] = jnp.broadcast_to(n_new, (SUBLANES, DH))
        m_new = b_col[L - 1:L, :] + M_end
        m_ref[0, h] = jnp.broadcast_to(m_new, (SUBLANES, DH))
        out_a = _head_norm(hh) * jax.nn.sigmoid(og)
        y_ref[0, :, lo:hi] = out_a * hg[:, lo:hi]

        cos2 = cos_ref[...]
        sin2 = sin_ref[...]
        rq = zr_ref[0, :, lo:hi]
        rk = zr_ref[0, :, W + lo:W + hi]
        rv = zr_ref[0, :, 2 * W + lo:2 * W + hi]
        rg = zr_ref[0, :, 3 * W + lo:3 * W + hi]
        qb = rq * cos2 + pltpu.roll(rq, DH // 2, axis=1) * sin2
        kb = (rk * cos2 + pltpu.roll(rk, DH // 2, axis=1) * sin2) * scale
        S = s_ref[0, h]
        sr = _dot_nt(qb, kb) * dmat_ref[h]
        o = _dot(sr, rv) + inter_ref[h] * _dot(qb, S)
        s_ref[0, h] = decay_ref[h] * S + _dot_tn(kb * wend_ref[h], rv)
        out_b = _head_norm(o) * (rg * jax.nn.sigmoid(rg))
        y_ref[0, :, W + lo:W + hi] = out_b * hg[:, W + lo:W + hi]


def _mixers(zm, zr, zg, zgt, cos2, sin2, dmat, inter, wend, decay, head_g, C0, n0, m0, S0,
            chunk, t_valid):
    B, T, _ = zm.shape
    nC = T // chunk
    D = 2 * N_HEADS * DH
    full = lambda a: pl.BlockSpec(a.shape, lambda b, c: (0,) * a.ndim)
    st4 = pl.BlockSpec((1, N_HEADS, DH, DH), lambda b, c: (b, 0, 0, 0))
    st3 = pl.BlockSpec((1, N_HEADS, SUBLANES, DH), lambda b, c: (b, 0, 0, 0))
    return pl.pallas_call(
        functools.partial(_mixers_kernel, chunk=chunk, t_valid=t_valid),
        grid=(B, nC),
        in_specs=[pl.BlockSpec((1, chunk, zm.shape[2]), lambda b, c: (b, c, 0)),
                  pl.BlockSpec((1, chunk, zr.shape[2]), lambda b, c: (b, c, 0)),
                  pl.BlockSpec((1, chunk, LANES), lambda b, c: (b, c, 0)),
                  pl.BlockSpec((1, SUBLANES, chunk), lambda b, c: (b, 0, c)),
                  pl.BlockSpec((chunk, DH), lambda b, c: (c, 0)),
                  pl.BlockSpec((chunk, DH), lambda b, c: (c, 0)),
                  full(dmat), full(inter), full(wend), full(decay), full(head_g),
                  st4, st3, st3, st4],
        out_specs=[pl.BlockSpec((1, chunk, D), lambda b, c: (b, c, 0)), st4, st3, st3, st4],
        out_shape=[jax.ShapeDtypeStruct((B, T, D), F32),
                   jax.ShapeDtypeStruct(C0.shape, F32),
                   jax.ShapeDtypeStruct(n0.shape, F32),
                   jax.ShapeDtypeStruct(m0.shape, F32),
                   jax.ShapeDtypeStruct(S0.shape, F32)],
        compiler_params=pltpu.CompilerParams(dimension_semantics=("parallel", "arbitrary"),
                                             vmem_limit_bytes=VMEM_LIMIT),
        name="mixers",
    )(zm, zr, zg, zgt, cos2, sin2, dmat, inter, wend, decay, head_g, C0, n0, m0, S0)


def _argmax_blocks(vals, tags, extras=()):
    v, t = list(vals), list(tags)
    ex = [list(e) for e in extras]
    while len(v) > 1:
        nv, nt, ne = [], [], [[] for _ in ex]
        for a in range(0, len(v) - 1, 2):
            take = v[a] >= v[a + 1]
            nv.append(jnp.where(take, v[a], v[a + 1]))
            nt.append(jnp.where(take, t[a], t[a + 1]))
            for q, e in enumerate(ex):
                ne[q].append(jnp.where(take, e[a], e[a + 1]))
        if len(v) % 2:
            nv.append(v[-1])
            nt.append(t[-1])
            for q, e in enumerate(ex):
                ne[q].append(e[-1])
        v, t, ex = nv, nt, ne
    m = jnp.max(v[0], axis=0, keepdims=True)
    tag = jnp.min(jnp.where(v[0] == m, t[0], jnp.int32(2 ** 30)), axis=0, keepdims=True)
    outs = [jnp.sum(jnp.where(t[0] == tag, e[0], 0), axis=0, keepdims=True) for e in ex]
    return m, tag, outs


def _topk_sorted(s, k):
    n, tok = s.shape
    nb = n // SUBLANES
    sub = lax.broadcasted_iota(jnp.int32, (SUBLANES, tok), 0)
    blocks = [s[b * SUBLANES:(b + 1) * SUBLANES] for b in range(nb)]
    tags = [sub + b * SUBLANES for b in range(nb)]
    v_rows, i_rows = [], []
    v_blk = [jnp.zeros((SUBLANES, tok), F32) for _ in range(k // SUBLANES)]
    i_blk = [jnp.zeros((SUBLANES, tok), jnp.int32) for _ in range(k // SUBLANES)]
    for r in range(k):
        m, i, _ = _argmax_blocks(blocks, tags)
        v_rows.append(m)
        i_rows.append(i)
        at = sub == (r % SUBLANES)
        v_blk[r // SUBLANES] = jnp.where(at, m, v_blk[r // SUBLANES])
        i_blk[r // SUBLANES] = jnp.where(at, i, i_blk[r // SUBLANES])
        blocks = [jnp.where(t == i, -jnp.inf, b) for b, t in zip(blocks, tags)]
    return v_rows, i_rows, v_blk, i_blk


def _product_topk(top0, top1, nk):
    K = PEER_TOPK
    v0_rows, i0_rows, v0_blk, i0_blk = top0
    v1_rows, i1_rows, v1_blk, i1_blk = top1
    tok = v0_rows[0].shape[1]
    sub = lax.broadcasted_iota(jnp.int32, (SUBLANES, tok), 0)
    vals, flats, experts = [], [], []
    for i in range(SUBLANES):
        for jb in range(K // SUBLANES):
            limit = K // (i + 1) - jb * SUBLANES
            if limit <= 0:
                continue
            blk = v0_rows[i] + v1_blk[jb]
            if limit < SUBLANES:
                blk = jnp.where(sub < limit, blk, -jnp.inf)
            vals.append(blk)
            flats.append(sub + (i * K + jb * SUBLANES))
            experts.append(i0_rows[i] * nk + i1_blk[jb])
    vals.append(v0_blk[1] + v1_rows[0])
    flats.append((sub + SUBLANES) * K)
    experts.append(i0_blk[1] * nk + i1_rows[0])
    tv, te = [], []
    for _ in range(K):
        m, f, (e,) = _argmax_blocks(vals, flats, extras=(experts,))
        tv.append(m)
        te.append(e)
        vals = [jnp.where(t == f, -jnp.inf, b) for b, t in zip(vals, flats)]
    return tv, te


def _post_kernel(x_ref, y_ref, g1_ref, sh2_ref, sc2_ref, wout_ref, l1g_ref, l1b_ref, wpq_ref,
                 keys_ref, x1_ref, h2_ref, e_ref, g_ref, *, alpha):
    mix = _dot(y_ref[0], wout_ref[...])
    x1 = _ln(alpha * x_ref[0] + g1_ref[0] * mix, LN_EPS) * l1g_ref[...] + l1b_ref[...]
    x1_ref[0] = x1
    h2 = _ln(x1, LN_EPS) * (1.0 + sc2_ref[0]) + sh2_ref[0]
    chunks = h2.shape[1] // LANES
    for j in range(chunks):
        h2_ref[pl.ds(j, h2.shape[0], stride=chunks), :] = h2[:, j * LANES:(j + 1) * LANES]
    qall = _dot(h2, wpq_ref[...])
    nk = keys_ref.shape[1]
    K = PEER_TOPK
    e_rows, g_rows = [], []
    for h in range(PEER_HEADS):
        tops = []
        for c in range(2):
            lo = (h * 2 + c) * DH
            st = _dot_nt(keys_ref[c], qall[:, lo:lo + DH])
            tops.append(_topk_sorted(st, K))
        tv, te = _product_topk(tops[0], tops[1], nk)
        tv = jnp.concatenate(tv, axis=0)
        ex = jnp.exp(tv - tv[0:1, :])
        g_rows.append(ex / jnp.sum(ex, axis=0, keepdims=True))
        e_rows.append(jnp.concatenate(te, axis=0))
    e_all = jnp.concatenate(e_rows, axis=0) * ROW_SUB
    g_all = jnp.concatenate(g_rows, axis=0)
    e_ref[0] = pltpu.bitcast(pltpu.bitcast(e_all, F32).T, jnp.int32)
    g_ref[0] = g_all.T


def _post(x, ymix, g1, sh2, sc2, w_out, ln1_g, ln1_b, w_pq, keys, alpha, tm):
    B, T, D = x.shape
    npair = PEER_HEADS * PEER_TOPK
    full = lambda a: pl.BlockSpec(a.shape, lambda b, i: (0,) * a.ndim)
    tok = lambda w: pl.BlockSpec((1, tm, w), lambda b, i: (b, i, 0))
    bat = pl.BlockSpec((1, 1, D), lambda b, i: (b, 0, 0))
    return pl.pallas_call(
        functools.partial(_post_kernel, alpha=alpha),
        grid=(B, T // tm),
        in_specs=[tok(D), tok(D), bat, bat, bat, full(w_out), full(ln1_g), full(ln1_b),
                  full(w_pq), full(keys)],
        out_specs=[tok(D), pl.BlockSpec((tm * D // LANES, LANES), lambda b, i: (b * (T // tm) + i, 0)),
                   tok(npair), tok(npair)],
        out_shape=[jax.ShapeDtypeStruct((B, T, D), F32),
                   jax.ShapeDtypeStruct((B * T * D // LANES, LANES), F32),
                   jax.ShapeDtypeStruct((B, T, npair), jnp.int32),
                   jax.ShapeDtypeStruct((B, T, npair), F32)],
        compiler_params=pltpu.CompilerParams(dimension_semantics=("parallel", "parallel"),
                                             vmem_limit_bytes=VMEM_LIMIT),
        name="post",
    )(x, ymix, g1, sh2, sc2, w_out, ln1_g, ln1_b, w_pq, keys)


def _unpack_pair(words):
    lo = pltpu.bitcast(words << 16, F32)
    hi = pltpu.bitcast(words & jnp.int32(-65536), F32)
    return lo, hi


def _gelu_exact(x):
    return 0.5 * x * (1.0 + lax.erf(x * (1.0 / math.sqrt(2.0))))


def _index_copy(e_hbm, e_scrs, sems, step, s):
    words = e_scrs[s].shape[0]
    return pltpu.make_async_copy(e_hbm.at[pl.ds((step * 2 + s) * words, words)], e_scrs[s], sems.at[s])


def _with_prefetched_indices(e_hbm, e_scrs, sems, process):
    step = pl.program_id(0)

    @pl.when(step == 0)
    def _():
        _index_copy(e_hbm, e_scrs, sems, step, 0).start()

    _index_copy(e_hbm, e_scrs, sems, step, 1).start()
    _index_copy(e_hbm, e_scrs, sems, step, 0).wait()
    process(0, e_scrs[0])
    _index_copy(e_hbm, e_scrs, sems, step, 1).wait()

    @pl.when(step + 1 < pl.num_programs(0))
    def _():
        _index_copy(e_hbm, e_scrs, sems, step + 1, 0).start()

    process(1, e_scrs[1])


def _gather_pair_words(tbl_ref, e_idx, k, q, half, sub):
    ra = pl.multiple_of(e_idx.at[pl.ds(q * sub, sub)][k], ROW_SUB)
    rb = pl.multiple_of(e_idx.at[pl.ds((half + q) * sub, sub)][k], ROW_SUB)
    return jnp.concatenate([tbl_ref[pl.ds(ra, ROW_SUB), :], tbl_ref[pl.ds(rb, ROW_SUB), :]], axis=0)


def _peer_u_kernel(e_hbm, x_ref, g_ref, tbl_ref, w_ref, slot_a, slot_b, act_ref, e_scr0, e_scr1, sems,
                   *, tile):
    npair = PEER_HEADS * PEER_TOPK
    half = npair // 2
    sub = tile // 2
    lane = lax.broadcasted_iota(jnp.int32, (npair, LANES), 1)

    def process(s, e_idx):
        first = s * sub

        def gather(k, slot):
            t = first + k
            x8 = x_ref[pl.ds(pl.multiple_of(t * SUBLANES, SUBLANES), SUBLANES), :]
            xlo = jnp.concatenate([x8[0:ROW_SUB], x8[0:ROW_SUB]], axis=0)
            xhi = jnp.concatenate([x8[ROW_SUB:], x8[ROW_SUB:]], axis=0)
            for q in range(half):
                lo, hi = _unpack_pair(_gather_pair_words(tbl_ref, e_idx, k, q, half, sub))
                slot[pl.ds(q, SUBLANES, stride=SLOT_STRIDE), :] = lo * xlo + hi * xhi

        def reduce(k, slot):
            parts = []
            for g in range(2):
                part = slot[pl.ds(g * ROW_SUB * SLOT_STRIDE, half), :]
                for j in range(1, ROW_SUB):
                    part = part + slot[pl.ds((g * ROW_SUB + j) * SLOT_STRIDE, half), :]
                parts.append(part)
            part = jnp.concatenate(parts, axis=0)
            act = jnp.sum(part, axis=1, keepdims=True)
            act_ref[...] = jnp.where(lane == first + k, act, act_ref[...])

        gather(0, slot_a)

        slots = (slot_a, slot_b)

        def token_group(i, carry):
            k0 = PEER_UNROLL * i
            for j in range(PEER_UNROLL):
                nxt = k0 + j + 1 if j + 1 < PEER_UNROLL else jnp.minimum(k0 + PEER_UNROLL, sub - 1)
                gather(nxt, slots[(j + 1) % 2])
                reduce(k0 + j, slots[j % 2])
            return carry

        lax.fori_loop(0, sub // PEER_UNROLL, token_group, 0)

    act_ref[...] = jnp.zeros_like(act_ref)
    _with_prefetched_indices(e_hbm, (e_scr0, e_scr1), sems, process)
    w_ref[...] = g_ref[...] * _gelu_exact(act_ref[...].T)


def _peer_u(eflat, x8, gw, table, tile):
    n, npair = gw.shape
    slot_rows = (SUBLANES - 1) * SLOT_STRIDE + npair // 2
    slot_rows = -(-slot_rows // SUBLANES) * SUBLANES
    tok = pl.BlockSpec((tile, npair), lambda i: (i, 0))
    return pl.pallas_call(
        functools.partial(_peer_u_kernel, tile=tile),
        grid=(n // tile,),
        in_specs=[pl.BlockSpec(memory_space=pl.ANY),
                  pl.BlockSpec((tile * SUBLANES, LANES), lambda i: (i, 0)),
                  tok,
                  pl.BlockSpec(memory_space=pltpu.VMEM)],
        out_specs=tok,
        out_shape=jax.ShapeDtypeStruct((n, npair), F32),
        scratch_shapes=[pltpu.VMEM((slot_rows, LANES), F32),
                        pltpu.VMEM((slot_rows, LANES), F32),
                        pltpu.VMEM((npair, LANES), F32),
                        pltpu.SMEM((tile // 2 * npair,), jnp.int32),
                        pltpu.SMEM((tile // 2 * npair,), jnp.int32),
                        pltpu.SemaphoreType.DMA((2,))],
        compiler_params=pltpu.CompilerParams(dimension_semantics=("arbitrary",),
                                             vmem_limit_bytes=VMEM_LIMIT),
        name="peer_u",
    )(eflat, x8, gw, table)


def _peer_v_kernel(e_hbm, w_ref, tbl_ref, o_ref, wb_a, wb_b, e_scr0, e_scr1, sems, *, tile):
    npair = PEER_HEADS * PEER_TOPK
    half = npair // 2
    sub = tile // 2
    nacc = 4
    top = lax.broadcasted_iota(jnp.int32, (SUBLANES, LANES), 0) < ROW_SUB

    def process(s, e_idx):
        first = s * sub

        def spread(k, wb):
            row = w_ref[pl.ds(first + k, 1), :]
            wb[...] = jnp.broadcast_to(row, (LANES, npair)).T

        def accumulate(k, wb):
            acc_lo = [jnp.zeros((SUBLANES, LANES), F32) for _ in range(nacc)]
            acc_hi = [jnp.zeros((SUBLANES, LANES), F32) for _ in range(nacc)]
            for q in range(half):
                lo, hi = _unpack_pair(_gather_pair_words(tbl_ref, e_idx, k, q, half, sub))
                w =jnp.where(top, wb[pl.ds(q, 1), :], wb[pl.ds(half + q, 1), :])
                acc_lo[q % nacc] = acc_lo[q % nacc] + w * lo
                acc_hi[q % nacc] = acc_hi[q % nacc] + w * hi
            lo = (acc_lo[0] + acc_lo[1]) + (acc_lo[2] + acc_lo[3])
            hi = (acc_hi[0] + acc_hi[1]) + (acc_hi[2] + acc_hi[3])
            lo = lo[0:ROW_SUB] + lo[ROW_SUB:]
            hi = hi[0:ROW_SUB] + hi[ROW_SUB:]
            rows = pl.ds(pl.multiple_of((first + k) * SUBLANES, SUBLANES), SUBLANES)
            o_ref[rows, :] = jnp.concatenate([lo, hi], axis=0)

        spread(0, wb_a)

        wbs = (wb_a, wb_b)

        def token_group(i, carry):
            k0 = PEER_UNROLL * i
            for j in range(PEER_UNROLL):
                nxt = k0 + j + 1 if j + 1 < PEER_UNROLL else jnp.minimum(k0 + PEER_UNROLL, sub - 1)
                spread(nxt, wbs[(j + 1) % 2])
                accumulate(k0 + j, wbs[j % 2])
            return carry

        lax.fori_loop(0, sub // PEER_UNROLL, token_group, 0)

    _with_prefetched_indices(e_hbm, (e_scr0, e_scr1), sems, process)


def _peer_v(eflat, w2, table, tile):
    n, npair = w2.shape
    return pl.pallas_call(
        functools.partial(_peer_v_kernel, tile=tile),
        grid=(n // tile,),
        in_specs=[pl.BlockSpec(memory_space=pl.ANY),
                  pl.BlockSpec((tile, npair), lambda i: (i, 0)),
                  pl.BlockSpec(memory_space=pltpu.VMEM)],
        out_specs=pl.BlockSpec((tile * SUBLANES, LANES), lambda i: (i, 0)),
        out_shape=jax.ShapeDtypeStruct((n * SUBLANES, LANES), F32),
        scratch_shapes=[pltpu.VMEM((npair, LANES), F32),
                        pltpu.VMEM((npair, LANES), F32),
                        pltpu.SMEM((tile // 2 * npair,), jnp.int32),
                        pltpu.SMEM((tile // 2 * npair,), jnp.int32),
                        pltpu.SemaphoreType.DMA((2,))],
        compiler_params=pltpu.CompilerParams(dimension_semantics=("arbitrary",),
                                             vmem_limit_bytes=VMEM_LIMIT),
        name="peer_v",
    )(eflat, w2, table)


def _pack_table(t):
    bits = lax.bitcast_convert_type(t.astype(BF16), jnp.uint16).astype(jnp.uint32)
    words = bits[:, :ROW_WORDS] | (bits[:, ROW_WORDS:] << 16)
    return lax.bitcast_convert_type(words, jnp.int32).reshape(t.shape[0] * ROW_SUB, LANES)


def _final_kernel(x_ref, p_ref, g2_ref, lg_ref, lb_ref, o_ref, *, alpha):
    tm, D = x_ref.shape[1], x_ref.shape[2]
    chunks = D // LANES
    peer = jnp.concatenate([p_ref[pl.ds(j, tm, stride=chunks), :] for j in range(chunks)], axis=1)
    o_ref[0] = _ln(alpha * x_ref[0] + g2_ref[0] * peer, LN_EPS) * lg_ref[...] + lb_ref[...]


def _final(x1, peer, g2, ln2_g, ln2_b, alpha, tm):
    B, T, D = x1.shape
    tok = pl.BlockSpec((1, tm, D), lambda b, i: (b, i, 0))
    vec = pl.BlockSpec((1, D), lambda b, i: (0, 0))
    return pl.pallas_call(
        functools.partial(_final_kernel, alpha=alpha),
        grid=(B, T // tm),
        in_specs=[tok, pl.BlockSpec((tm * D // LANES, LANES), lambda b, i: (b * (T // tm) + i, 0)),
                  pl.BlockSpec((1, 1, D), lambda b, i: (b, 0, 0)), vec, vec],
        out_specs=tok,
        out_shape=jax.ShapeDtypeStruct((B, T, D), F32),
        compiler_params=pltpu.CompilerParams(dimension_semantics=("parallel", "parallel")),
        name="final",
    )(x1, peer, g2, ln2_g, ln2_b)


def _retention_consts(chunk, t_valid):
    lg = jnp.log1p(-(2.0 ** (-5.0 - jnp.arange(N_HEADS, dtype=F32))))
    pos = jnp.arange(chunk, dtype=F32)
    diff = pos[:, None] - pos[None, :]
    dmat = jnp.where(diff >= 0, jnp.exp(lg[:, None, None] * jnp.maximum(diff, 0.0)), 0.0)
    inter = jnp.exp(lg[:, None] * (pos + 1.0))[..., None]
    wend = jnp.where(pos < t_valid, jnp.exp(lg[:, None] * (t_valid - 1.0 - pos)), 0.0)[..., None]
    decay = jnp.exp(lg * t_valid)[:, None, None]
    return dmat, inter, wend, decay


def _rotary_tables(T, pos0):
    half = DH // 2
    inv = ROPE_BASE ** (-jnp.arange(half, dtype=F32) / half)
    ang = (pos0 + jnp.arange(T, dtype=F32))[:, None] * inv[None, :]
    cos, sin = jnp.cos(ang), jnp.sin(ang)
    return jnp.concatenate([cos, cos], axis=-1), jnp.concatenate([-sin, sin], axis=-1)


def _layer(x, c, pos0, t_valid, C0, n0, m0, S0, p, alpha):
    B, T, D = x.shape
    W = N_HEADS * DH
    mod = _ada(c, p["w_ada"], p["b_ada"])
    sh1, sc1, g1, sh2, sc2, g2 = [m[:, None, :] for m in jnp.split(mod, 6, axis=-1)]

    w_in, b_in = p["w_in"], p["b_in"]
    g_off = 4 * W
    r_off = g_off + 2 * N_HEADS
    wm = w_in[:, :g_off].astype(BF16)
    wr = w_in[:, r_off:].astype(BF16)
    wg = jnp.pad(w_in[:, g_off:r_off], ((0, 0), (0, LANES - 2 * N_HEADS)))
    bg = jnp.pad(b_in[g_off:r_off], (0, LANES - 2 * N_HEADS))[None, :]
    wgt = w_in[:, g_off:r_off].T
    bgt = b_in[g_off:r_off][:, None]
    tm = min(T, 512)
    zm, zr, zg, zgt = _inproj(x, sh1, sc1, wm, b_in[None, :g_off], wr, b_in[None, r_off:],
                              wg, bg, wgt, bgt, tm)

    chunk = min(T, MIX_CHUNK)
    tv = min(t_valid, chunk) if T == chunk else chunk
    cos2, sin2 = _rotary_tables(T, pos0)
    dmat, inter, wend, decay = _retention_consts(chunk, tv)
    bc = lambda a: jnp.broadcast_to(a[..., None, :], a.shape[:-1] + (SUBLANES, a.shape[-1]))
    n0b = bc(n0.astype(F32))
    m0b = jnp.broadcast_to(m0.astype(F32)[..., None, None], m0.shape + (SUBLANES, DH))
    ymix, C1, n1, m1, S1 = _mixers(zm, zr, zg, zgt, cos2, sin2, dmat, inter, wend, decay,
                                   p["head_g"][None, :], C0.astype(F32), n0b, m0b, S0.astype(F32),
                                   chunk, tv)
    n1 = n1[:, :, 0, :]
    m1 = m1[:, :, 0, 0]

    x1, h2, eidx, gw = _post(x, ymix, g1, sh2, sc2, p["w_out"].astype(BF16), p["ln1_g"][None, :],
                             p["ln1_b"][None, :], p["w_pq"].astype(BF16), p["sub_keys"], alpha,
                             POST_TILE)
    npair = PEER_HEADS * PEER_TOPK
    chunks = D // LANES
    tile = PEER_TILE
    sub = tile // 2
    nv = B * t_valid
    n = -(-nv // tile) * tile

    def valid_rows(a, per_token):
        if t_valid == T and n == nv:
            return a
        a = a.reshape(B, T * per_token, a.shape[-1])[:, :t_valid * per_token]
        return jnp.pad(a.reshape(nv * per_token, a.shape[-1]), ((0, (n - nv) * per_token), (0, 0)))

    x8 = valid_rows(h2, chunks)
    e2 = valid_rows(eidx.reshape(B * T, npair), 1)
    gw2 = valid_rows(gw.reshape(B * T, npair), 1)
    eflat = e2.reshape(n // sub, sub, npair).transpose(0, 2, 1).reshape(n * npair)
    wts = _peer_u(eflat, x8, gw2, _pack_table(p["expert_u"]), tile)
    peer = _peer_v(eflat, wts, _pack_table(p["expert_v"]), tile)
    if not (t_valid == T and n == nv):
        peer = peer[:nv * chunks].reshape(B, t_valid * chunks, LANES)
        peer = jnp.pad(peer, ((0, 0), (0, (T - t_valid) * chunks), (0, 0))).reshape(B * T * chunks, LANES)
    y = _final(x1, peer, g2, p["ln2_g"][None, :], p["ln2_b"][None, :], alpha, min(T, 512))
    return y, (C1, n1, m1, S1)


def kernel(x_prompt, x_sample, c_prompt, c_sample, state_mlstm_C, state_mlstm_n, state_mlstm_m,
           state_ret_S, w_ada, b_ada, w_in, b_in, head_g, w_out, ln1_g, ln1_b, w_pq, sub_keys,
           expert_u, expert_v, ln2_g, ln2_b):
    depth = w_ada.shape[0]
    alpha = (2.0 * depth) ** 0.25
    params = dict(w_ada=w_ada, b_ada=b_ada, w_in=w_in, b_in=b_in, head_g=head_g, w_out=w_out,
                  ln1_g=ln1_g, ln1_b=ln1_b, w_pq=w_pq, sub_keys=sub_keys, expert_u=expert_u,
                  expert_v=expert_v, ln2_g=ln2_g, ln2_b=ln2_b)

    def run(x, c, pos0, C0, n0, m0, S0):
        B, T, D = x.shape
        t_valid = T
        if T % LANES:
            x = jnp.pad(x, ((0, 0), (0, LANES - T % LANES), (0, 0)))
        Cs, ns, ms, Ss = [], [], [], []
        for l in range(depth):
            pl_ = {k: v[l] for k, v in params.items()}
            x, (C1, n1, m1, S1) = _layer(x, c, pos0, t_valid, C0[l], n0[l], m0[l], S0[l], pl_, alpha)
            Cs.append(C1); ns.append(n1); ms.append(m1); Ss.append(S1)
        return x[:, :T], jnp.stack(Cs), jnp.stack(ns), jnp.stack(ms), jnp.stack(Ss)

    B = x_prompt.shape[0]
    H = state_mlstm_C.shape[2]
    zC = jnp.zeros((depth, B, H, DH, DH), F32)
    zn = jnp.zeros((depth, B, H, DH), F32)
    zm = jnp.zeros((depth, B, H), F32)
    zS = jnp.zeros((depth, B, H, DH, DH), F32)
    past_len = 1024
    y_p, C_p, n_p, m_p, S_p = run(x_prompt, c_prompt, 0, zC, zn, zm, zS)
    y_s, C_s, n_s, m_s, S_s = run(x_sample, c_sample, past_len, state_mlstm_C, state_mlstm_n,
                                  state_mlstm_m, state_ret_S)
    return (y_p, y_s, C_p, n_p, m_p, S_p, C_s, n_s, m_s, S_s)
```

```python
import functools
import math

import jax
import jax.numpy as jnp
from jax import lax
from jax.experimental import pallas as pl
from jax.experimental.pallas import tpu as pltpu

F32 = jnp.float32
BF16 = jnp.bfloat16
HIGHEST = lax.Precision.HIGHEST

LN_EPS = 1e-5
HEAD_EPS = 1e-6
ROPE_BASE = 10000.0
N_HEADS = 4
DH = 128
LANES = 128
SUBLANES = 8
PEER_HEADS = 8
PEER_TOPK = 16
MIX_CHUNK = 256
POST_TILE = 128
PEER_TILE = 128
PEER_UNROLL = 16
ROW_WORDS = 512
ROW_SUB = ROW_WORDS // LANES
SLOT_STRIDE = 65
MASKED_GATE = -1e30
VMEM_LIMIT = 56 * 1024 * 1024


def _ln(x, eps):
    mu = jnp.mean(x, axis=-1, keepdims=True)
    xc = x - mu
    var = jnp.mean(xc * xc, axis=-1, keepdims=True)
    return xc * lax.rsqrt(var + eps)


def _dot(a, b):
    return jnp.dot(a.astype(BF16), b.astype(BF16), preferred_element_type=F32)


def _dot_nt(a, b):
    return lax.dot_general(a.astype(BF16), b.astype(BF16), (((1,), (1,)), ((), ())),
                           preferred_element_type=F32)


def _dot_tn(a, b):
    return lax.dot_general(a.astype(BF16), b.astype(BF16), (((0,), (0,)), ((), ())),
                           preferred_element_type=F32)


def _log_sigmoid(x):
    return jnp.minimum(x, 0.0) - jnp.log1p(jnp.exp(-jnp.abs(x)))


def _ada_kernel(c_ref, w_ref, b_ref, o_ref):
    c = c_ref[...]
    s = c * jax.nn.sigmoid(c)
    o_ref[...] = jnp.dot(s, w_ref[...], preferred_element_type=F32, precision=HIGHEST) + b_ref[...]


def _ada(c, w_ada, b_ada):
    B, D = c.shape
    n = w_ada.shape[1]
    return pl.pallas_call(
        _ada_kernel,
        grid=(n // D,),
        in_specs=[pl.BlockSpec((B, D), lambda j: (0, 0)),
                  pl.BlockSpec((D, D), lambda j: (0, j)),
                  pl.BlockSpec((1, D), lambda j: (0, j))],
        out_specs=pl.BlockSpec((B, D), lambda j: (0, j)),
        out_shape=jax.ShapeDtypeStruct((B, n), F32),
        name="ada",
    )(c, w_ada, b_ada.reshape(1, n))


def _inproj_kernel(x_ref, sh_ref, sc_ref, wm_ref, bm_ref, wr_ref, br_ref, wg_ref, bg_ref,
                   wgt_ref, bgt_ref, zm_ref, zr_ref, zg_ref, zgt_ref):
    h = _ln(x_ref[0], LN_EPS) * (1.0 + sc_ref[0]) + sh_ref[0]
    hb = h.astype(BF16)
    zm_ref[0] = jnp.dot(hb, wm_ref[...], preferred_element_type=F32) + bm_ref[...]
    zr_ref[0] = jnp.dot(hb, wr_ref[...], preferred_element_type=F32) + br_ref[...]
    zg_ref[0] = jnp.dot(h, wg_ref[...], preferred_element_type=F32, precision=HIGHEST) + bg_ref[...]
    zgt_ref[0] = lax.dot_general(wgt_ref[...], h, (((1,), (1,)), ((), ())),
                                 preferred_element_type=F32, precision=HIGHEST) + bgt_ref[...]


def _inproj(x, sh, sc, wm, bm, wr, br, wg, bg, wgt, bgt, tm):
    B, T, D = x.shape
    wm_n, wr_n = wm.shape[1], wr.shape[1]
    full = lambda a: pl.BlockSpec(a.shape, lambda b, i: (0,) * a.ndim)
    return pl.pallas_call(
        _inproj_kernel,
        grid=(B, T // tm),
        in_specs=[pl.BlockSpec((1, tm, D), lambda b, i: (b, i, 0)),
                  pl.BlockSpec((1, 1, D), lambda b, i: (b, 0, 0)),
                  pl.BlockSpec((1, 1, D), lambda b, i: (b, 0, 0)),
                  full(wm), full(bm), full(wr), full(br), full(wg), full(bg), full(wgt), full(bgt)],
        out_specs=[pl.BlockSpec((1, tm, wm_n), lambda b, i: (b, i, 0)),
                   pl.BlockSpec((1, tm, wr_n), lambda b, i: (b, i, 0)),
                   pl.BlockSpec((1, tm, LANES), lambda b, i: (b, i, 0)),
                   pl.BlockSpec((1, SUBLANES, tm), lambda b, i: (b, 0, i))],
        out_shape=[jax.ShapeDtypeStruct((B, T, wm_n), F32),
                   jax.ShapeDtypeStruct((B, T, wr_n), F32),
                   jax.ShapeDtypeStruct((B, T, LANES), F32),
                   jax.ShapeDtypeStruct((B, SUBLANES, T), F32)],
        compiler_params=pltpu.CompilerParams(dimension_semantics=("parallel", "parallel"),
                                             vmem_limit_bytes=VMEM_LIMIT),
        name="inproj",
    )(x, sh, sc, wm, bm, wr, br, wg, bg, wgt, bgt)


def _head_norm(t):
    return _ln(t, HEAD_EPS)


def _mixers_kernel(zm_ref, zr_ref, zg_ref, zgt_ref, cos_ref, sin_ref, dmat_ref, inter_ref,
                   wend_ref, decay_ref, hg_ref, c0_ref, n0_ref, m0_ref, s0_ref,
                   y_ref, c_ref, n_ref, m_ref, s_ref, *, chunk, t_valid):
    L = chunk
    W = N_HEADS * DH

    @pl.when(pl.program_id(1) == 0)
    def _():
        c_ref[...] = c0_ref[...]
        n_ref[...] = n0_ref[...]
        m_ref[...] = m0_ref[...]
        s_ref[...] = s0_ref[...]

    row = lax.broadcasted_iota(jnp.int32, (L, L), 0)
    col = lax.broadcasted_iota(jnp.int32, (L, L), 1)
    tri = col <= row
    zg = zg_ref[0]
    zgt = zgt_ref[0]
    hg = hg_ref[...]
    if t_valid < L:
        valid_col = lax.broadcasted_iota(jnp.int32, (L, 1), 0) < t_valid
        valid_row = lax.broadcasted_iota(jnp.int32, (1, L), 1) < t_valid
    scale = DH ** -0.5

    for h in range(N_HEADS):
        lo, hi = h * DH, (h + 1) * DH
        q = zm_ref[0, :, lo:hi]
        k = zm_ref[0, :, W + lo:W + hi] * scale
        v = zm_ref[0, :, 2 * W + lo:2 * W + hi]
        og = zm_ref[0, :, 3 * W + lo:3 * W + hi]
        ig_col = zg[:, h:h + 1]
        lf_col = _log_sigmoid(zg[:, N_HEADS + h:N_HEADS + h + 1])
        ig_row = zgt[h:h + 1, :]
        lf_row = _log_sigmoid(zgt[N_HEADS + h:N_HEADS + h + 1, :])
        if t_valid < L:
            ig_col = jnp.where(valid_col, ig_col, MASKED_GATE)
            ig_row = jnp.where(valid_row, ig_row, MASKED_GATE)
            lf_col = jnp.where(valid_col, lf_col, 0.0)
            lf_row = jnp.where(valid_row, lf_row, 0.0)
        C = c_ref[0, h]
        n_row = n_ref[0, h, 0:1, :]
        m_prev = m_ref[0, h, 0:1, 0:1]

        b_col = jnp.sum(jnp.where(tri, lf_row, 0.0), axis=1, keepdims=True)
        b_row = jnp.sum(jnp.where(row <= col, lf_col, 0.0), axis=0, keepdims=True)
        a_col = ig_col - b_col
        a_row = ig_row - b_row
        cm_col = jnp.max(jnp.where(tri, a_row, -jnp.inf), axis=1, keepdims=True)
        M_col = jnp.maximum(m_prev, cm_col)
        dm = jnp.exp(jnp.where(tri, a_row - M_col, -jnp.inf))
        inter = jnp.exp(m_prev - M_col)
        s = _dot_nt(q, k) * dm
        num = inter * _dot(q, C) + _dot(s, v)
        den = inter * jnp.sum(q * n_row, axis=1, keepdims=True) + jnp.sum(s, axis=1, keepdims=True)
        hh = num / jnp.maximum(jnp.abs(den), jnp.exp(-(b_col + M_col)))
        M_end = M_col[L - 1:L, :]
        w_end = jnp.exp(a_col - M_end)
        decay = jnp.exp(m_prev - M_end)
        kw = k * w_end
        c_ref[0, h] = decay * C + _dot_tn(kw, v)
        n_new = decay * n_row + jnp.sum(kw, axis=0, keepdims=True)
        n_ref[0, h] = jnp.broadcast_to(n_new, (SUBLANES, DH))
        m_new = b_col[L - 1:L, :] + M_end
        m_ref[0, h] = jnp.broadcast_to(m_new, (SUBLANES, DH))
        out_a = _head_norm(hh) * jax.nn.sigmoid(og)
        y_ref[0, :, lo:hi] = out_a * hg[:, lo:hi]

        cos2 = cos_ref[...]
        sin2 = sin_ref[...]
        rq = zr_ref[0, :, lo:hi]
        rk = zr_ref[0, :, W + lo:W + hi]
        rv = zr_ref[0, :, 2 * W + lo:2 * W + hi]
        rg = zr_ref[0, :, 3 * W + lo:3 * W + hi]
        qb = rq * cos2 + pltpu.roll(rq, DH // 2, axis=1) * sin2
        kb = (rk * cos2 + pltpu.roll(rk, DH // 2, axis=1) * sin2) * scale
        S = s_ref[0, h]
        sr = _dot_nt(qb, kb) * dmat_ref[h]
        o = _dot(sr, rv) + inter_ref[h] * _dot(qb, S)
        s_ref[0, h] = decay_ref[h] * S + _dot_tn(kb * wend_ref[h], rv)
        out_b = _head_norm(o) * (rg * jax.nn.sigmoid(rg))
        y_ref[0, :, W + lo:W + hi] = out_b * hg[:, W + lo:W + hi]


def _mixers(zm, zr, zg, zgt, cos2, sin2, dmat, inter, wend, decay, head_g, C0, n0, m0, S0,
            chunk, t_valid):
    B, T, _ = zm.shape
    nC = T // chunk
    D = 2 * N_HEADS * DH
    full = lambda a: pl.BlockSpec(a.shape, lambda b, c: (0,) * a.ndim)
    st4 = pl.BlockSpec((1, N_HEADS, DH, DH), lambda b, c: (b, 0, 0, 0))
    st3 = pl.BlockSpec((1, N_HEADS, SUBLANES, DH), lambda b, c: (b, 0, 0, 0))
    return pl.pallas_call(
        functools.partial(_mixers_kernel, chunk=chunk, t_valid=t_valid),
        grid=(B, nC),
        in_specs=[pl.BlockSpec((1, chunk, zm.shape[2]), lambda b, c: (b, c, 0)),
                  pl.BlockSpec((1, chunk, zr.shape[2]), lambda b, c: (b, c, 0)),
                  pl.BlockSpec((1, chunk, LANES), lambda b, c: (b, c, 0)),
                  pl.BlockSpec((1, SUBLANES, chunk), lambda b, c: (b, 0, c)),
                  pl.BlockSpec((chunk, DH), lambda b, c: (c, 0)),
                  pl.BlockSpec((chunk, DH), lambda b, c: (c, 0)),
                  full(dmat), full(inter), full(wend), full(decay), full(head_g),
                  st4, st3, st3, st4],
        out_specs=[pl.BlockSpec((1, chunk, D), lambda b, c: (b, c, 0)), st4, st3, st3, st4],
        out_shape=[jax.ShapeDtypeStruct((B, T, D), F32),
                   jax.ShapeDtypeStruct(C0.shape, F32),
                   jax.ShapeDtypeStruct(n0.shape, F32),
                   jax.ShapeDtypeStruct(m0.shape, F32),
                   jax.ShapeDtypeStruct(S0.shape, F32)],
        compiler_params=pltpu.CompilerParams(dimension_semantics=("parallel", "arbitrary"),
                                             vmem_limit_bytes=VMEM_LIMIT),
        name="mixers",
    )(zm, zr, zg, zgt, cos2, sin2, dmat, inter, wend, decay, head_g, C0, n0, m0, S0)


def _argmax_blocks(vals, tags, extras=()):
    v, t = list(vals), list(tags)
    ex = [list(e) for e in extras]
    while len(v) > 1:
        nv, nt, ne = [], [], [[] for _ in ex]
        for a in range(0, len(v) - 1, 2):
            take = v[a] >= v[a + 1]
            nv.append(jnp.where(take, v[a], v[a + 1]))
            nt.append(jnp.where(take, t[a], t[a + 1]))
            for q, e in enumerate(ex):
                ne[q].append(jnp.where(take, e[a], e[a + 1]))
        if len(v) % 2:
            nv.append(v[-1])
            nt.append(t[-1])
            for q, e in enumerate(ex):
                ne[q].append(e[-1])
        v, t, ex = nv, nt, ne
    m = jnp.max(v[0], axis=0, keepdims=True)
    tag = jnp.min(jnp.where(v[0] == m, t[0], jnp.int32(2 ** 30)), axis=0, keepdims=True)
    outs = [jnp.sum(jnp.where(t[0] == tag, e[0], 0), axis=0, keepdims=True) for e in ex]
    return m, tag, outs


def _topk_sorted(s, k):
    n, tok = s.shape
    nb = n // SUBLANES
    sub = lax.broadcasted_iota(jnp.int32, (SUBLANES, tok), 0)
    blocks = [s[b * SUBLANES:(b + 1) * SUBLANES] for b in range(nb)]
    tags = [sub + b * SUBLANES for b in range(nb)]
    v_rows, i_rows = [], []
    v_blk = [jnp.zeros((SUBLANES, tok), F32) for _ in range(k // SUBLANES)]
    i_blk = [jnp.zeros((SUBLANES, tok), jnp.int32) for _ in range(k // SUBLANES)]
    for r in range(k):
        m, i, _ = _argmax_blocks(blocks, tags)
        v_rows.append(m)
        i_rows.append(i)
        at = sub == (r % SUBLANES)
        v_blk[r // SUBLANES] = jnp.where(at, m, v_blk[r // SUBLANES])
        i_blk[r // SUBLANES] = jnp.where(at, i, i_blk[r // SUBLANES])
        blocks = [jnp.where(t == i, -jnp.inf, b) for b, t in zip(blocks, tags)]
    return v_rows, i_rows, v_blk, i_blk


def _product_topk(top0, top1, nk):
    K = PEER_TOPK
    v0_rows, i0_rows, v0_blk, i0_blk = top0
    v1_rows, i1_rows, v1_blk, i1_blk = top1
    tok = v0_rows[0].shape[1]
    sub = lax.broadcasted_iota(jnp.int32, (SUBLANES, tok), 0)
    vals, flats, experts = [], [], []
    for i in range(SUBLANES):
        for jb in range(K // SUBLANES):
            limit = K // (i + 1) - jb * SUBLANES
            if limit <= 0:
                continue
            blk = v0_rows[i] + v1_blk[jb]
            if limit < SUBLANES:
                blk = jnp.where(sub < limit, blk, -jnp.inf)
            vals.append(blk)
            flats.append(sub + (i * K + jb * SUBLANES))
            experts.append(i0_rows[i] * nk + i1_blk[jb])
    vals.append(v0_blk[1] + v1_rows[0])
    flats.append((sub + SUBLANES) * K)
    experts.append(i0_blk[1] * nk + i1_rows[0])
    tv, te = [], []
    for _ in range(K):
        m, f, (e,) = _argmax_blocks(vals, flats, extras=(experts,))
        tv.append(m)
        te.append(e)
        vals = [jnp.where(t == f, -jnp.inf, b) for b, t in zip(vals, flats)]
    return tv, te


def _post_kernel(x_ref, y_ref, g1_ref, sh2_ref, sc2_ref, wout_ref, l1g_ref, l1b_ref, wpq_ref,
                 keys_ref, x1_ref, h2_ref, e_ref, g_ref, *, alpha):
    mix = _dot(y_ref[0], wout_ref[...])
    x1 = _ln(alpha * x_ref[0] + g1_ref[0] * mix, LN_EPS) * l1g_ref[...] + l1b_ref[...]
    x1_ref[0] = x1
    h2 = _ln(x1, LN_EPS) * (1.0 + sc2_ref[0]) + sh2_ref[0]
    chunks = h2.shape[1] // LANES
    for j in range(chunks):
        h2_ref[pl.ds(j, h2.shape[0], stride=chunks), :] = h2[:, j * LANES:(j + 1) * LANES]
    qall = _dot(h2, wpq_ref[...])
    nk = keys_ref.shape[1]
    K = PEER_TOPK
    e_rows, g_rows = [], []
    for h in range(PEER_HEADS):
        tops = []
        for c in range(2):
            lo = (h * 2 + c) * DH
            st = _dot_nt(keys_ref[c], qall[:, lo:lo + DH])
            tops.append(_topk_sorted(st, K))
        tv, te = _product_topk(tops[0], tops[1], nk)
        tv = jnp.concatenate(tv, axis=0)
        ex = jnp.exp(tv - tv[0:1, :])
        g_rows.append(ex / jnp.sum(ex, axis=0, keepdims=True))
        e_rows.append(jnp.concatenate(te, axis=0))
    e_all = jnp.concatenate(e_rows, axis=0) * ROW_SUB
    g_all = jnp.concatenate(g_rows, axis=0)
    e_ref[0] = pltpu.bitcast(pltpu.bitcast(e_all, F32).T, jnp.int32)
    g_ref[0] = g_all.T


def _post(x, ymix, g1, sh2, sc2, w_out, ln1_g, ln1_b, w_pq, keys, alpha, tm):
    B, T, D = x.shape
    npair = PEER_HEADS * PEER_TOPK
    full = lambda a: pl.BlockSpec(a.shape, lambda b, i: (0,) * a.ndim)
    tok = lambda w: pl.BlockSpec((1, tm, w), lambda b, i: (b, i, 0))
    bat = pl.BlockSpec((1, 1, D), lambda b, i: (b, 0, 0))
    return pl.pallas_call(
        functools.partial(_post_kernel, alpha=alpha),
        grid=(B, T // tm),
        in_specs=[tok(D), tok(D), bat, bat, bat, full(w_out), full(ln1_g), full(ln1_b),
                  full(w_pq), full(keys)],
        out_specs=[tok(D), pl.BlockSpec((tm * D // LANES, LANES), lambda b, i: (b * (T // tm) + i, 0)),
                   tok(npair), tok(npair)],
        out_shape=[jax.ShapeDtypeStruct((B, T, D), F32),
                   jax.ShapeDtypeStruct((B * T * D // LANES, LANES), F32),
                   jax.ShapeDtypeStruct((B, T, npair), jnp.int32),
                   jax.ShapeDtypeStruct((B, T, npair), F32)],
        compiler_params=pltpu.CompilerParams(dimension_semantics=("parallel", "parallel"),
                                             vmem_limit_bytes=VMEM_LIMIT),
        name="post",
    )(x, ymix, g1, sh2, sc2, w_out, ln1_g, ln1_b, w_pq, keys)


def _unpack_pair(words):
    lo = pltpu.bitcast(words << 16, F32)
    hi = pltpu.bitcast(words & jnp.int32(-65536), F32)
    return lo, hi


def _gelu_exact(x):
    return 0.5 * x * (1.0 + lax.erf(x * (1.0 / math.sqrt(2.0))))


def _index_copy(e_hbm, e_scrs, sems, step, s):
    words = e_scrs[s].shape[0]
    return pltpu.make_async_copy(e_hbm.at[pl.ds((step * 2 + s) * words, words)], e_scrs[s], sems.at[s])


def _with_prefetched_indices(e_hbm, e_scrs, sems, process):
    step = pl.program_id(0)

    @pl.when(step == 0)
    def _():
        _index_copy(e_hbm, e_scrs, sems, step, 0).start()

    _index_copy(e_hbm, e_scrs, sems, step, 1).start()
    _index_copy(e_hbm, e_scrs, sems, step, 0).wait()
    process(0, e_scrs[0])
    _index_copy(e_hbm, e_scrs, sems, step, 1).wait()

    @pl.when(step + 1 < pl.num_programs(0))
    def _():
        _index_copy(e_hbm, e_scrs, sems, step + 1, 0).start()

    process(1, e_scrs[1])


def _gather_pair_words(tbl_ref, e_idx, k, q, half, sub):
    ra = pl.multiple_of(e_idx.at[pl.ds(q * sub, sub)][k], ROW_SUB)
    rb = pl.multiple_of(e_idx.at[pl.ds((half + q) * sub, sub)][k], ROW_SUB)
    return jnp.concatenate([tbl_ref[pl.ds(ra, ROW_SUB), :], tbl_ref[pl.ds(rb, ROW_SUB), :]], axis=0)


def _peer_u_kernel(e_hbm, x_ref, g_ref, tbl_ref, w_ref, slot_a, slot_b, act_ref, e_scr0, e_scr1, sems,
                   *, tile):
    npair = PEER_HEADS * PEER_TOPK
    half = npair // 2
    sub = tile // 2
    lane = lax.broadcasted_iota(jnp.int32, (npair, LANES), 1)

    def process(s, e_idx):
        first = s * sub

        def gather(k, slot):
            t = first + k
            x8 = x_ref[pl.ds(pl.multiple_of(t * SUBLANES, SUBLANES), SUBLANES), :]
            xlo = jnp.concatenate([x8[0:ROW_SUB], x8[0:ROW_SUB]], axis=0)
            xhi = jnp.concatenate([x8[ROW_SUB:], x8[ROW_SUB:]], axis=0)
            for q in range(half):
                lo, hi = _unpack_pair(_gather_pair_words(tbl_ref, e_idx, k, q, half, sub))
                slot[pl.ds(q, SUBLANES, stride=SLOT_STRIDE), :] = lo * xlo + hi * xhi

        def reduce(k, slot):
            parts = []
            for g in range(2):
                part = slot[pl.ds(g * ROW_SUB * SLOT_STRIDE, half), :]
                for j in range(1, ROW_SUB):
                    part = part + slot[pl.ds((g * ROW_SUB + j) * SLOT_STRIDE, half), :]
                parts.append(part)
            part = jnp.concatenate(parts, axis=0)
            act = jnp.sum(part, axis=1, keepdims=True)
            act_ref[...] = jnp.where(lane == first + k, act, act_ref[...])

        gather(0, slot_a)

        slots = (slot_a, slot_b)

        def token_group(i, carry):
            k0 = PEER_UNROLL * i
            for j in range(PEER_UNROLL):
                nxt = k0 + j + 1 if j + 1 < PEER_UNROLL else jnp.minimum(k0 + PEER_UNROLL, sub - 1)
                gather(nxt, slots[(j + 1) % 2])
                reduce(k0 + j, slots[j % 2])
            return carry

        lax.fori_loop(0, sub // PEER_UNROLL, token_group, 0)

    act_ref[...] = jnp.zeros_like(act_ref)
    _with_prefetched_indices(e_hbm, (e_scr0, e_scr1), sems, process)
    w_ref[...] = g_ref[...] * _gelu_exact(act_ref[...].T)


def _peer_u(eflat, x8, gw, table, tile):
    n, npair = gw.shape
    slot_rows = (SUBLANES - 1) * SLOT_STRIDE + npair // 2
    slot_rows = -(-slot_rows // SUBLANES) * SUBLANES
    tok = pl.BlockSpec((tile, npair), lambda i: (i, 0))
    return pl.pallas_call(
        functools.partial(_peer_u_kernel, tile=tile),
        grid=(n // tile,),
        in_specs=[pl.BlockSpec(memory_space=pl.ANY),
                  pl.BlockSpec((tile * SUBLANES, LANES), lambda i: (i, 0)),
                  tok,
                  pl.BlockSpec(memory_space=pltpu.VMEM)],
        out_specs=tok,
        out_shape=jax.ShapeDtypeStruct((n, npair), F32),
        scratch_shapes=[pltpu.VMEM((slot_rows, LANES), F32),
                        pltpu.VMEM((slot_rows, LANES), F32),
                        pltpu.VMEM((npair, LANES), F32),
                        pltpu.SMEM((tile // 2 * npair,), jnp.int32),
                        pltpu.SMEM((tile // 2 * npair,), jnp.int32),
                        pltpu.SemaphoreType.DMA((2,))],
        compiler_params=pltpu.CompilerParams(dimension_semantics=("arbitrary",),
                                             vmem_limit_bytes=VMEM_LIMIT),
        name="peer_u",
    )(eflat, x8, gw, table)


def _peer_v_kernel(e_hbm, w_ref, tbl_ref, o_ref, wb_a, wb_b, e_scr0, e_scr1, sems, *, tile):
    npair = PEER_HEADS * PEER_TOPK
    half = npair // 2
    sub = tile // 2
    nacc = 4
    top = lax.broadcasted_iota(jnp.int32, (SUBLANES, LANES), 0) < ROW_SUB

    def process(s, e_idx):
        first = s * sub

        def spread(k, wb):
            row = w_ref[pl.ds(first + k, 1), :]
            wb[...] = jnp.broadcast_to(row, (LANES, npair)).T

        def accumulate(k, wb):
            acc_lo = [jnp.zeros((SUBLANES, LANES), F32) for _ in range(nacc)]
            acc_hi = [jnp.zeros((SUBLANES, LANES), F32) for _ in range(nacc)]
            for q in range(half):
                lo, hi = _unpack_pair(_gather_pair_words(tbl_ref, e_idx, k, q, half, sub))
                w =jnp.where(top, wb[pl.ds(q, 1), :], wb[pl.ds(half + q, 1), :])
                acc_lo[q % nacc] = acc_lo[q % nacc] + w * lo
                acc_hi[q % nacc] = acc_hi[q % nacc] + w * hi
            lo = (acc_lo[0] + acc_lo[1]) + (acc_lo[2] + acc_lo[3])
            hi = (acc_hi[0] + acc_hi[1]) + (acc_hi[2] + acc_hi[3])
            lo = lo[0:ROW_SUB] + lo[ROW_SUB:]
            hi = hi[0:ROW_SUB] + hi[ROW_SUB:]
            rows = pl.ds(pl.multiple_of((first + k) * SUBLANES, SUBLANES), SUBLANES)
            o_ref[rows, :] = jnp.concatenate([lo, hi], axis=0)

        spread(0, wb_a)

        wbs = (wb_a, wb_b)

        def token_group(i, carry):
            k0 = PEER_UNROLL * i
            for j in range(PEER_UNROLL):
                nxt = k0 + j + 1 if j + 1 < PEER_UNROLL else jnp.minimum(k0 + PEER_UNROLL, sub - 1)
                spread(nxt, wbs[(j + 1) % 2])
                accumulate(k0 + j, wbs[j % 2])
            return carry

        lax.fori_loop(0, sub // PEER_UNROLL, token_group, 0)

    _with_prefetched_indices(e_hbm, (e_scr0, e_scr1), sems, process)


def _peer_v(eflat, w2, table, tile):
    n, npair = w2.shape
    return pl.pallas_call(
        functools.partial(_peer_v_kernel, tile=tile),
        grid=(n // tile,),
        in_specs=[pl.BlockSpec(memory_space=pl.ANY),
                  pl.BlockSpec((tile, npair), lambda i: (i, 0)),
                  pl.BlockSpec(memory_space=pltpu.VMEM)],
        out_specs=pl.BlockSpec((tile * SUBLANES, LANES), lambda i: (i, 0)),
        out_shape=jax.ShapeDtypeStruct((n * SUBLANES, LANES), F32),
        scratch_shapes=[pltpu.VMEM((npair, LANES), F32),
                        pltpu.VMEM((npair, LANES), F32),
                        pltpu.SMEM((tile // 2 * npair,), jnp.int32),
                        pltpu.SMEM((tile // 2 * npair,), jnp.int32),
                        pltpu.SemaphoreType.DMA((2,))],
        compiler_params=pltpu.CompilerParams(dimension_semantics=("arbitrary",),
                                             vmem_limit_bytes=VMEM_LIMIT),
        name="peer_v",
    )(eflat, w2, table)


def _pack_table(t):
    bits = lax.bitcast_convert_type(t.astype(BF16), jnp.uint16).astype(jnp.uint32)
    words = bits[:, :ROW_WORDS] | (bits[:, ROW_WORDS:] << 16)
    return lax.bitcast_convert_type(words, jnp.int32).reshape(t.shape[0] * ROW_SUB, LANES)


def _final_kernel(x_ref, p_ref, g2_ref, lg_ref, lb_ref, o_ref, *, alpha):
    tm, D = x_ref.shape[1], x_ref.shape[2]
    chunks = D // LANES
    peer = jnp.concatenate([p_ref[pl.ds(j, tm, stride=chunks), :] for j in range(chunks)], axis=1)
    o_ref[0] = _ln(alpha * x_ref[0] + g2_ref[0] * peer, LN_EPS) * lg_ref[...] + lb_ref[...]


def _final(x1, peer, g2, ln2_g, ln2_b, alpha, tm):
    B, T, D = x1.shape
    tok = pl.BlockSpec((1, tm, D), lambda b, i: (b, i, 0))
    vec = pl.BlockSpec((1, D), lambda b, i: (0, 0))
    return pl.pallas_call(
        functools.partial(_final_kernel, alpha=alpha),
        grid=(B, T // tm),
        in_specs=[tok, pl.BlockSpec((tm * D // LANES, LANES), lambda b, i: (b * (T // tm) + i, 0)),
                  pl.BlockSpec((1, 1, D), lambda b, i: (b, 0, 0)), vec, vec],
        out_specs=tok,
        out_shape=jax.ShapeDtypeStruct((B, T, D), F32),
        compiler_params=pltpu.CompilerParams(dimension_semantics=("parallel", "parallel")),
        name="final",
    )(x1, peer, g2, ln2_g, ln2_b)


def _retention_consts(chunk, t_valid):
    lg = jnp.log1p(-(2.0 ** (-5.0 - jnp.arange(N_HEADS, dtype=F32))))
    pos = jnp.arange(chunk, dtype=F32)
    diff = pos[:, None] - pos[None, :]
    dmat = jnp.where(diff >= 0, jnp.exp(lg[:, None, None] * jnp.maximum(diff, 0.0)), 0.0)
    inter = jnp.exp(lg[:, None] * (pos + 1.0))[..., None]
    wend = jnp.where(pos < t_valid, jnp.exp(lg[:, None] * (t_valid - 1.0 - pos)), 0.0)[..., None]
    decay = jnp.exp(lg * t_valid)[:, None, None]
    return dmat, inter, wend, decay


def _rotary_tables(T, pos0):
    half = DH // 2
    inv = ROPE_BASE ** (-jnp.arange(half, dtype=F32) / half)
    ang = (pos0 + jnp.arange(T, dtype=F32))[:, None] * inv[None, :]
    cos, sin = jnp.cos(ang), jnp.sin(ang)
    return jnp.concatenate([cos, cos], axis=-1), jnp.concatenate([-sin, sin], axis=-1)


def _layer(x, c, pos0, t_valid, C0, n0, m0, S0, p, alpha):
    B, T, D = x.shape
    W = N_HEADS * DH
    mod = _ada(c, p["w_ada"], p["b_ada"])
    sh1, sc1, g1, sh2, sc2, g2 = [m[:, None, :] for m in jnp.split(mod, 6, axis=-1)]

    w_in, b_in = p["w_in"], p["b_in"]
    g_off = 4 * W
    r_off = g_off + 2 * N_HEADS
    wm = w_in[:, :g_off].astype(BF16)
    wr = w_in[:, r_off:].astype(BF16)
    wg = jnp.pad(w_in[:, g_off:r_off], ((0, 0), (0, LANES - 2 * N_HEADS)))
    bg = jnp.pad(b_in[g_off:r_off], (0, LANES - 2 * N_HEADS))[None, :]
    wgt = w_in[:, g_off:r_off].T
    bgt = b_in[g_off:r_off][:, None]
    tm = min(T, 512)
    zm, zr, zg, zgt = _inproj(x, sh1, sc1, wm, b_in[None, :g_off], wr, b_in[None, r_off:],
                              wg, bg, wgt, bgt, tm)

    chunk = min(T, MIX_CHUNK)
    tv = min(t_valid, chunk) if T == chunk else chunk
    cos2, sin2 = _rotary_tables(T, pos0)
    dmat, inter, wend, decay = _retention_consts(chunk, tv)
    bc = lambda a: jnp.broadcast_to(a[..., None, :], a.shape[:-1] + (SUBLANES, a.shape[-1]))
    n0b = bc(n0.astype(F32))
    m0b = jnp.broadcast_to(m0.astype(F32)[..., None, None], m0.shape + (SUBLANES, DH))
    ymix, C1, n1, m1, S1 = _mixers(zm, zr, zg, zgt, cos2, sin2, dmat, inter, wend, decay,
                                   p["head_g"][None, :], C0.astype(F32), n0b, m0b, S0.astype(F32),
                                   chunk, tv)
    n1 = n1[:, :, 0, :]
    m1 = m1[:, :, 0, 0]

    x1, h2, eidx, gw = _post(x, ymix, g1, sh2, sc2, p["w_out"].astype(BF16), p["ln1_g"][None, :],
                             p["ln1_b"][None, :], p["w_pq"].astype(BF16), p["sub_keys"], alpha,
                             POST_TILE)
    npair = PEER_HEADS * PEER_TOPK
    chunks = D // LANES
    tile = PEER_TILE
    sub = tile // 2
    nv = B * t_valid
    n = -(-nv // tile) * tile

    def valid_rows(a, per_token):
        if t_valid == T and n == nv:
            return a
        a = a.reshape(B, T * per_token, a.shape[-1])[:, :t_valid * per_token]
        return jnp.pad(a.reshape(nv * per_token, a.shape[-1]), ((0, (n - nv) * per_token), (0, 0)))

    x8 = valid_rows(h2, chunks)
    e2 = valid_rows(eidx.reshape(B * T, npair), 1)
    gw2 = valid_rows(gw.reshape(B * T, npair), 1)
    eflat = e2.reshape(n // sub, sub, npair).transpose(0, 2, 1).reshape(n * npair)
    wts = _peer_u(eflat, x8, gw2, _pack_table(p["expert_u"]), tile)
    peer = _peer_v(eflat, wts, _pack_table(p["expert_v"]), tile)
    if not (t_valid == T and n == nv):
        peer = peer[:nv * chunks].reshape(B, t_valid * chunks, LANES)
        peer = jnp.pad(peer, ((0, 0), (0, (T - t_valid) * chunks), (0, 0))).reshape(B * T * chunks, LANES)
    y = _final(x1, peer, g2, p["ln2_g"][None, :], p["ln2_b"][None, :], alpha, min(T, 512))
    return y, (C1, n1, m1, S1)


def kernel(x_prompt, x_sample, c_prompt, c_sample, state_mlstm_C, state_mlstm_n, state_mlstm_m,
           state_ret_S, w_ada, b_ada, w_in, b_in, head_g, w_out, ln1_g, ln1_b, w_pq, sub_keys,
           expert_u, expert_v, ln2_g, ln2_b):
    depth = w_ada.shape[0]
    alpha = (2.0 * depth) ** 0.25
    params = dict(w_ada=w_ada, b_ada=b_ada, w_in=w_in, b_in=b_in, head_g=head_g, w_out=w_out,
                  ln1_g=ln1_g, ln1_b=ln1_b, w_pq=w_pq, sub_keys=sub_keys, expert_u=expert_u,
                  expert_v=expert_v, ln2_g=ln2_g, ln2_b=ln2_b)

    def run(x, c, pos0, C0, n0, m0, S0):
        B, T, D = x.shape
        t_valid = T
        if T % LANES:
            x = jnp.pad(x, ((0, 0), (0, LANES - T % LANES), (0, 0)))
        Cs, ns, ms, Ss = [], [], [], []
        for l in range(depth):
            pl_ = {k: v[l] for k, v in params.items()}
            x, (C1, n1, m1, S1) = _layer(x, c, pos0, t_valid, C0[l], n0[l], m0[l], S0[l], pl_, alpha)
            Cs.append(C1); ns.append(n1); ms.append(m1); Ss.append(S1)
        return x[:, :T], jnp.stack(Cs), jnp.stack(ns), jnp.stack(ms), jnp.stack(Ss)

    B = x_prompt.shape[0]
    H = state_mlstm_C.shape[2]
    zC = jnp.zeros((depth, B, H, DH, DH), F32)
    zn = jnp.zeros((depth, B, H, DH), F32)
    zm = jnp.zeros((depth, B, H), F32)
    zS = jnp.zeros((depth, B, H, DH, DH), F32)
    past_len = 1024
    y_p, C_p, n_p, m_p, S_p = run(x_prompt, c_prompt, 0, zC, zn, zm, zS)
    y_s, C_s, n_s, m_s, S_s = run(x_sample, c_sample, past_len, state_mlstm_C, state_mlstm_n,
                                  state_mlstm_m, state_ret_S)
    return (y_p, y_s, C_p, n_p, m_p, S_p, C_s, n_s, m_s, S_s)
```

```python
import functools
import math

import jax
import jax.numpy as jnp
from jax import lax
from jax.experimental import pallas as pl
from jax.experimental.pallas import tpu as pltpu

F32 = jnp.float32
BF16 = jnp.bfloat16
HIGHEST = lax.Precision.HIGHEST

LN_EPS = 1e-5
HEAD_EPS = 1e-6
ROPE_BASE = 10000.0
N_HEADS = 4
DH = 128
LANES = 128
SUBLANES = 8
PEER_HEADS = 8
PEER_TOPK = 16
MIX_CHUNK = 256
POST_TILE = 128
PEER_TILE = 128
PEER_UNROLL = 32
ROW_WORDS = 512
ROW_SUB = ROW_WORDS // LANES
SLOT_STRIDE = 65
MASKED_GATE = -1e30
VMEM_LIMIT = 56 * 1024 * 1024


def _ln(x, eps):
    mu = jnp.mean(x, axis=-1, keepdims=True)
    xc = x - mu
    var = jnp.mean(xc * xc, axis=-1, keepdims=True)
    return xc * lax.rsqrt(var + eps)


def _dot(a, b):
    return jnp.dot(a.astype(BF16), b.astype(BF16), preferred_element_type=F32)


def _dot_nt(a, b):
    return lax.dot_general(a.astype(BF16), b.astype(BF16), (((1,), (1,)), ((), ())),
                           preferred_element_type=F32)


def _dot_tn(a, b):
    return lax.dot_general(a.astype(BF16), b.astype(BF16), (((0,), (0,)), ((), ())),
                           preferred_element_type=F32)


def _log_sigmoid(x):
    return jnp.minimum(x, 0.0) - jnp.log1p(jnp.exp(-jnp.abs(x)))


def _ada_kernel(c_ref, w_ref, b_ref, o_ref):
    c = c_ref[...]
    s = c * jax.nn.sigmoid(c)
    o_ref[...] = jnp.dot(s, w_ref[...], preferred_element_type=F32, precision=HIGHEST) + b_ref[...]


def _ada(c, w_ada, b_ada):
    B, D = c.shape
    n = w_ada.shape[1]
    return pl.pallas_call(
        _ada_kernel,
        grid=(n // D,),
        in_specs=[pl.BlockSpec((B, D), lambda j: (0, 0)),
                  pl.BlockSpec((D, D), lambda j: (0, j)),
                  pl.BlockSpec((1, D), lambda j: (0, j))],
        out_specs=pl.BlockSpec((B, D), lambda j: (0, j)),
        out_shape=jax.ShapeDtypeStruct((B, n), F32),
        name="ada",
    )(c, w_ada, b_ada.reshape(1, n))


def _inproj_kernel(x_ref, sh_ref, sc_ref, wm_ref, bm_ref, wr_ref, br_ref, wg_ref, bg_ref,
                   wgt_ref, bgt_ref, zm_ref, zr_ref, zg_ref, zgt_ref):
    h = _ln(x_ref[0], LN_EPS) * (1.0 + sc_ref[0]) + sh_ref[0]
    hb = h.astype(BF16)
    zm_ref[0] = jnp.dot(hb, wm_ref[...], preferred_element_type=F32) + bm_ref[...]
    zr_ref[0] = jnp.dot(hb, wr_ref[...], preferred_element_type=F32) + br_ref[...]
    zg_ref[0] = jnp.dot(h, wg_ref[...], preferred_element_type=F32, precision=HIGHEST) + bg_ref[...]
    zgt_ref[0] = lax.dot_general(wgt_ref[...], h, (((1,), (1,)), ((), ())),
                                 preferred_element_type=F32, precision=HIGHEST) + bgt_ref[...]


def _inproj(x, sh, sc, wm, bm, wr, br, wg, bg, wgt, bgt, tm):
    B, T, D = x.shape
    wm_n, wr_n = wm.shape[1], wr.shape[1]
    full = lambda a: pl.BlockSpec(a.shape, lambda b, i: (0,) * a.ndim)
    return pl.pallas_call(
        _inproj_kernel,
        grid=(B, T // tm),
        in_specs=[pl.BlockSpec((1, tm, D), lambda b, i: (b, i, 0)),
                  pl.BlockSpec((1, 1, D), lambda b, i: (b, 0, 0)),
                  pl.BlockSpec((1, 1, D), lambda b, i: (b, 0, 0)),
                  full(wm), full(bm), full(wr), full(br), full(wg), full(bg), full(wgt), full(bgt)],
        out_specs=[pl.BlockSpec((1, tm, wm_n), lambda b, i: (b, i, 0)),
                   pl.BlockSpec((1, tm, wr_n), lambda b, i: (b, i, 0)),
                   pl.BlockSpec((1, tm, LANES), lambda b, i: (b, i, 0)),
                   pl.BlockSpec((1, SUBLANES, tm), lambda b, i: (b, 0, i))],
        out_shape=[jax.ShapeDtypeStruct((B, T, wm_n), F32),
                   jax.ShapeDtypeStruct((B, T, wr_n), F32),
                   jax.ShapeDtypeStruct((B, T, LANES), F32),
                   jax.ShapeDtypeStruct((B, SUBLANES, T), F32)],
        compiler_params=pltpu.CompilerParams(dimension_semantics=("parallel", "parallel"),
                                             vmem_limit_bytes=VMEM_LIMIT),
        name="inproj",
    )(x, sh, sc, wm, bm, wr, br, wg, bg, wgt, bgt)


def _head_norm(t):
    return _ln(t, HEAD_EPS)


def _mixers_kernel(zm_ref, zr_ref, zg_ref, zgt_ref, cos_ref, sin_ref, dmat_ref, inter_ref,
                   wend_ref, decay_ref, hg_ref, c0_ref, n0_ref, m0_ref, s0_ref,
                   y_ref, c_ref, n_ref, m_ref, s_ref, *, chunk, t_valid):
    L = chunk
    W = N_HEADS * DH

    @pl.when(pl.program_id(1) == 0)
    def _():
        c_ref[...] = c0_ref[...]
        n_ref[...] = n0_ref[...]
        m_ref[...] = m0_ref[...]
        s_ref[...] = s0_ref[...]

    row = lax.broadcasted_iota(jnp.int32, (L, L), 0)
    col = lax.broadcasted_iota(jnp.int32, (L, L), 1)
    tri = col <= row
    zg = zg_ref[0]
    zgt = zgt_ref[0]
    hg = hg_ref[...]
    if t_valid < L:
        valid_col = lax.broadcasted_iota(jnp.int32, (L, 1), 0) < t_valid
        valid_row = lax.broadcasted_iota(jnp.int32, (1, L), 1) < t_valid
    scale = DH ** -0.5

    for h in range(N_HEADS):
        lo, hi = h * DH, (h + 1) * DH
        q = zm_ref[0, :, lo:hi]
        k = zm_ref[0, :, W + lo:W + hi] * scale
        v = zm_ref[0, :, 2 * W + lo:2 * W + hi]
        og = zm_ref[0, :, 3 * W + lo:3 * W + hi]
        ig_col = zg[:, h:h + 1]
        lf_col = _log_sigmoid(zg[:, N_HEADS + h:N_HEADS + h + 1])
        ig_row = zgt[h:h + 1, :]
        lf_row = _log_sigmoid(zgt[N_HEADS + h:N_HEADS + h + 1, :])
        if t_valid < L:
            ig_col = jnp.where(valid_col, ig_col, MASKED_GATE)
            ig_row = jnp.where(valid_row, ig_row, MASKED_GATE)
            lf_col = jnp.where(valid_col, lf_col, 0.0)
            lf_row = jnp.where(valid_row, lf_row, 0.0)
        C = c_ref[0, h]
        n_row = n_ref[0, h, 0:1, :]
        m_prev = m_ref[0, h, 0:1, 0:1]

        b_col = jnp.sum(jnp.where(tri, lf_row, 0.0), axis=1, keepdims=True)
        b_row = jnp.sum(jnp.where(row <= col, lf_col, 0.0), axis=0, keepdims=True)
        a_col = ig_col - b_col
        a_row = ig_row - b_row
        cm_col = jnp.max(jnp.where(tri, a_row, -jnp.inf), axis=1, keepdims=True)
        M_col = jnp.maximum(m_prev, cm_col)
        dm = jnp.exp(jnp.where(tri, a_row - M_col, -jnp.inf))
        inter = jnp.exp(m_prev - M_col)
        s = _dot_nt(q, k) * dm
        num = inter * _dot(q, C) + _dot(s, v)
        den = inter * jnp.sum(q * n_row, axis=1, keepdims=True) + jnp.sum(s, axis=1, keepdims=True)
        hh = num / jnp.maximum(jnp.abs(den), jnp.exp(-(b_col + M_col)))
        M_end = M_col[L - 1:L, :]
        w_end = jnp.exp(a_col - M_end)
        decay = jnp.exp(m_prev - M_end)
        kw = k * w_end
        c_ref[0, h] = decay * C + _dot_tn(kw, v)
        n_new = decay * n_row + jnp.sum(kw, axis=0, keepdims=True)
        n_ref[0, h] = jnp.broadcast_to(n_new, (SUBLANES, DH))
        m_new = b_col[L - 1:L, :] + M_end
        m_ref[0, h] = jnp.broadcast_to(m_new, (SUBLANES, DH))
        out_a = _head_norm(hh) * jax.nn.sigmoid(og)
        y_ref[0, :, lo:hi] = out_a * hg[:, lo:hi]

        cos2 = cos_ref[...]
        sin2 = sin_ref[...]
        rq = zr_ref[0, :, lo:hi]
        rk = zr_ref[0, :, W + lo:W + hi]
        rv = zr_ref[0, :, 2 * W + lo:2 * W + hi]
        rg = zr_ref[0, :, 3 * W + lo:3 * W + hi]
        qb = rq * cos2 + pltpu.roll(rq, DH // 2, axis=1) * sin2
        kb = (rk * cos2 + pltpu.roll(rk, DH // 2, axis=1) * sin2) * scale
        S = s_ref[0, h]
        sr = _dot_nt(qb, kb) * dmat_ref[h]
        o = _dot(sr, rv) + inter_ref[h] * _dot(qb, S)
        s_ref[0, h] = decay_ref[h] * S + _dot_tn(kb * wend_ref[h], rv)
        out_b = _head_norm(o) * (rg * jax.nn.sigmoid(rg))
        y_ref[0, :, W + lo:W + hi] = out_b * hg[:, W + lo:W + hi]


def _mixers(zm, zr, zg, zgt, cos2, sin2, dmat, inter, wend, decay, head_g, C0, n0, m0, S0,
            chunk, t_valid):
    B, T, _ = zm.shape
    nC = T // chunk
    D = 2 * N_HEADS * DH
    full = lambda a: pl.BlockSpec(a.shape, lambda b, c: (0,) * a.ndim)
    st4 = pl.BlockSpec((1, N_HEADS, DH, DH), lambda b, c: (b, 0, 0, 0))
    st3 = pl.BlockSpec((1, N_HEADS, SUBLANES, DH), lambda b, c: (b, 0, 0, 0))
    return pl.pallas_call(
        functools.partial(_mixers_kernel, chunk=chunk, t_valid=t_valid),
        grid=(B, nC),
        in_specs=[pl.BlockSpec((1, chunk, zm.shape[2]), lambda b, c: (b, c, 0)),
                  pl.BlockSpec((1, chunk, zr.shape[2]), lambda b, c: (b, c, 0)),
                  pl.BlockSpec((1, chunk, LANES), lambda b, c: (b, c, 0)),
                  pl.BlockSpec((1, SUBLANES, chunk), lambda b, c: (b, 0, c)),
                  pl.BlockSpec((chunk, DH), lambda b, c: (c, 0)),
                  pl.BlockSpec((chunk, DH), lambda b, c: (c, 0)),
                  full(dmat), full(inter), full(wend), full(decay), full(head_g),
                  st4, st3, st3, st4],
        out_specs=[pl.BlockSpec((1, chunk, D), lambda b, c: (b, c, 0)), st4, st3, st3, st4],
        out_shape=[jax.ShapeDtypeStruct((B, T, D), F32),
                   jax.ShapeDtypeStruct(C0.shape, F32),
                   jax.ShapeDtypeStruct(n0.shape, F32),
                   jax.ShapeDtypeStruct(m0.shape, F32),
                   jax.ShapeDtypeStruct(S0.shape, F32)],
        compiler_params=pltpu.CompilerParams(dimension_semantics=("parallel", "arbitrary"),
                                             vmem_limit_bytes=VMEM_LIMIT),
        name="mixers",
    )(zm, zr, zg, zgt, cos2, sin2, dmat, inter, wend, decay, head_g, C0, n0, m0, S0)


def _argmax_blocks(vals, tags, extras=()):
    v, t = list(vals), list(tags)
    ex = [list(e) for e in extras]
    while len(v) > 1:
        nv, nt, ne = [], [], [[] for _ in ex]
        for a in range(0, len(v) - 1, 2):
            take = v[a] >= v[a + 1]
            nv.append(jnp.where(take, v[a], v[a + 1]))
            nt.append(jnp.where(take, t[a], t[a + 1]))
            for q, e in enumerate(ex):
                ne[q].append(jnp.where(take, e[a], e[a + 1]))
        if len(v) % 2:
            nv.append(v[-1])
            nt.append(t[-1])
            for q, e in enumerate(ex):
                ne[q].append(e[-1])
        v, t, ex = nv, nt, ne
    m = jnp.max(v[0], axis=0, keepdims=True)
    tag = jnp.min(jnp.where(v[0] == m, t[0], jnp.int32(2 ** 30)), axis=0, keepdims=True)
    outs = [jnp.sum(jnp.where(t[0] == tag, e[0], 0), axis=0, keepdims=True) for e in ex]
    return m, tag, outs


def _topk_sorted(s, k):
    n, tok = s.shape
    nb = n // SUBLANES
    sub = lax.broadcasted_iota(jnp.int32, (SUBLANES, tok), 0)
    blocks = [s[b * SUBLANES:(b + 1) * SUBLANES] for b in range(nb)]
    tags = [sub + b * SUBLANES for b in range(nb)]
    v_rows, i_rows = [], []
    v_blk = [jnp.zeros((SUBLANES, tok), F32) for _ in range(k // SUBLANES)]
    i_blk = [jnp.zeros((SUBLANES, tok), jnp.int32) for _ in range(k // SUBLANES)]
    for r in range(k):
        m, i, _ = _argmax_blocks(blocks, tags)
        v_rows.append(m)
        i_rows.append(i)
        at = sub == (r % SUBLANES)
        v_blk[r // SUBLANES] = jnp.where(at, m, v_blk[r // SUBLANES])
        i_blk[r // SUBLANES] = jnp.where(at, i, i_blk[r // SUBLANES])
        blocks = [jnp.where(t == i, -jnp.inf, b) for b, t in zip(blocks, tags)]
    return v_rows, i_rows, v_blk, i_blk


def _product_topk(top0, top1, nk):
    K = PEER_TOPK
    v0_rows, i0_rows, v0_blk, i0_blk = top0
    v1_rows, i1_rows, v1_blk, i1_blk = top1
    tok = v0_rows[0].shape[1]
    sub = lax.broadcasted_iota(jnp.int32, (SUBLANES, tok), 0)
    vals, flats, experts = [], [], []
    for i in range(SUBLANES):
        for jb in range(K // SUBLANES):
            limit = K // (i + 1) - jb * SUBLANES
            if limit <= 0:
                continue
            blk = v0_rows[i] + v1_blk[jb]
            if limit < SUBLANES:
                blk = jnp.where(sub < limit, blk, -jnp.inf)
            vals.append(blk)
            flats.append(sub + (i * K + jb * SUBLANES))
            experts.append(i0_rows[i] * nk + i1_blk[jb])
    vals.append(v0_blk[1] + v1_rows[0])
    flats.append((sub + SUBLANES) * K)
    experts.append(i0_blk[1] * nk + i1_rows[0])
    tv, te = [], []
    for _ in range(K):
        m, f, (e,) = _argmax_blocks(vals, flats, extras=(experts,))
        tv.append(m)
        te.append(e)
        vals = [jnp.where(t == f, -jnp.inf, b) for b, t in zip(vals, flats)]
    return tv, te


def _post_kernel(x_ref, y_ref, g1_ref, sh2_ref, sc2_ref, wout_ref, l1g_ref, l1b_ref, wpq_ref,
                 keys_ref, x1_ref, h2_ref, e_ref, g_ref, *, alpha):
    mix = _dot(y_ref[0], wout_ref[...])
    x1 = _ln(alpha * x_ref[0] + g1_ref[0] * mix, LN_EPS) * l1g_ref[...] + l1b_ref[...]
    x1_ref[0] = x1
    h2 = _ln(x1, LN_EPS) * (1.0 + sc2_ref[0]) + sh2_ref[0]
    chunks = h2.shape[1] // LANES
    for j in range(chunks):
        h2_ref[pl.ds(j, h2.shape[0], stride=chunks), :] = h2[:, j * LANES:(j + 1) * LANES]
    qall = _dot(h2, wpq_ref[...])
    nk = keys_ref.shape[1]
    K = PEER_TOPK
    e_rows, g_rows = [], []
    for h in range(PEER_HEADS):
        tops = []
        for c in range(2):
            lo = (h * 2 + c) * DH
            st = _dot_nt(keys_ref[c], qall[:, lo:lo + DH])
            tops.append(_topk_sorted(st, K))
        tv, te = _product_topk(tops[0], tops[1], nk)
        tv = jnp.concatenate(tv, axis=0)
        ex = jnp.exp(tv - tv[0:1, :])
        g_rows.append(ex / jnp.sum(ex, axis=0, keepdims=True))
        e_rows.append(jnp.concatenate(te, axis=0))
    e_all = jnp.concatenate(e_rows, axis=0) * ROW_SUB
    g_all = jnp.concatenate(g_rows, axis=0)
    e_ref[0] = pltpu.bitcast(pltpu.bitcast(e_all, F32).T, jnp.int32)
    g_ref[0] = g_all.T


def _post(x, ymix, g1, sh2, sc2, w_out, ln1_g, ln1_b, w_pq, keys, alpha, tm):
    B, T, D = x.shape
    npair = PEER_HEADS * PEER_TOPK
    full = lambda a: pl.BlockSpec(a.shape, lambda b, i: (0,) * a.ndim)
    tok = lambda w: pl.BlockSpec((1, tm, w), lambda b, i: (b, i, 0))
    bat = pl.BlockSpec((1, 1, D), lambda b, i: (b, 0, 0))
    return pl.pallas_call(
        functools.partial(_post_kernel, alpha=alpha),
        grid=(B, T // tm),
        in_specs=[tok(D), tok(D), bat, bat, bat, full(w_out), full(ln1_g), full(ln1_b),
                  full(w_pq), full(keys)],
        out_specs=[tok(D), pl.BlockSpec((tm * D // LANES, LANES), lambda b, i: (b * (T // tm) + i, 0)),
                   tok(npair), tok(npair)],
        out_shape=[jax.ShapeDtypeStruct((B, T, D), F32),
                   jax.ShapeDtypeStruct((B * T * D // LANES, LANES), F32),
                   jax.ShapeDtypeStruct((B, T, npair), jnp.int32),
                   jax.ShapeDtypeStruct((B, T, npair), F32)],
        compiler_params=pltpu.CompilerParams(dimension_semantics=("parallel", "parallel"),
                                             vmem_limit_bytes=VMEM_LIMIT),
        name="post",
    )(x, ymix, g1, sh2, sc2, w_out, ln1_g, ln1_b, w_pq, keys)


def _unpack_pair(words):
    lo = pltpu.bitcast(words << 16, F32)
    hi = pltpu.bitcast(words & jnp.int32(-65536), F32)
    return lo, hi


def _gelu_exact(x):
    return 0.5 * x * (1.0 + lax.erf(x * (1.0 / math.sqrt(2.0))))


def _index_copy(e_hbm, e_scrs, sems, step, s):
    words = e_scrs[s].shape[0]
    return pltpu.make_async_copy(e_hbm.at[pl.ds((step * 2 + s) * words, words)], e_scrs[s], sems.at[s])


def _with_prefetched_indices(e_hbm, e_scrs, sems, process):
    step = pl.program_id(0)

    @pl.when(step == 0)
    def _():
        _index_copy(e_hbm, e_scrs, sems, step, 0).start()

    _index_copy(e_hbm, e_scrs, sems, step, 1).start()
    _index_copy(e_hbm, e_scrs, sems, step, 0).wait()
    process(0, e_scrs[0])
    _index_copy(e_hbm, e_scrs, sems, step, 1).wait()

    @pl.when(step + 1 < pl.num_programs(0))
    def _():
        _index_copy(e_hbm, e_scrs, sems, step + 1, 0).start()

    process(1, e_scrs[1])


def _gather_pair_words(tbl_ref, e_idx, k, q, half, sub):
    ra = pl.multiple_of(e_idx.at[pl.ds(q * sub, sub)][k], ROW_SUB)
    rb = pl.multiple_of(e_idx.at[pl.ds((half + q) * sub, sub)][k], ROW_SUB)
    return jnp.concatenate([tbl_ref[pl.ds(ra, ROW_SUB), :], tbl_ref[pl.ds(rb, ROW_SUB), :]], axis=0)


def _peer_u_kernel(e_hbm, x_ref, g_ref, tbl_ref, w_ref, slot_a, slot_b, act_ref, e_scr0, e_scr1, sems,
                   *, tile):
    npair = PEER_HEADS * PEER_TOPK
    half = npair // 2
    sub = tile // 2
    lane = lax.broadcasted_iota(jnp.int32, (npair, LANES), 1)

    def process(s, e_idx):
        first = s * sub

        def gather(k, slot):
            t = first + k
            x8 = x_ref[pl.ds(pl.multiple_of(t * SUBLANES, SUBLANES), SUBLANES), :]
            xlo = jnp.concatenate([x8[0:ROW_SUB], x8[0:ROW_SUB]], axis=0)
            xhi = jnp.concatenate([x8[ROW_SUB:], x8[ROW_SUB:]], axis=0)
            for q in range(half):
                lo, hi = _unpack_pair(_gather_pair_words(tbl_ref, e_idx, k, q, half, sub))
                slot[pl.ds(q, SUBLANES, stride=SLOT_STRIDE), :] = lo * xlo + hi * xhi

        def reduce(k, slot):
            parts = []
            for g in range(2):
                part = slot[pl.ds(g * ROW_SUB * SLOT_STRIDE, half), :]
                for j in range(1, ROW_SUB):
                    part = part + slot[pl.ds((g * ROW_SUB + j) * SLOT_STRIDE, half), :]
                parts.append(part)
            part = jnp.concatenate(parts, axis=0)
            act = jnp.sum(part, axis=1, keepdims=True)
            act_ref[...] = jnp.where(lane == first + k, act, act_ref[...])

        gather(0, slot_a)

        slots = (slot_a, slot_b)

        def token_group(i, carry):
            k0 = PEER_UNROLL * i
            for j in range(PEER_UNROLL):
                nxt = k0 + j + 1 if j + 1 < PEER_UNROLL else jnp.minimum(k0 + PEER_UNROLL, sub - 1)
                gather(nxt, slots[(j + 1) % 2])
                reduce(k0 + j, slots[j % 2])
            return carry

        lax.fori_loop(0, sub // PEER_UNROLL, token_group, 0)

    act_ref[...] = jnp.zeros_like(act_ref)
    _with_prefetched_indices(e_hbm, (e_scr0, e_scr1), sems, process)
    act_ref[...] = g_ref[...].T * _gelu_exact(act_ref[...])
    top = lax.broadcasted_iota(jnp.int32, (SUBLANES, LANES), 0) < ROW_SUB
    wide = jnp.concatenate([jnp.where(top, act_ref[pl.ds(q, 1), :], act_ref[pl.ds(half + q, 1), :])
                            for q in range(half)], axis=0)
    w_ref[...] = wide.T


def _peer_u(eflat, x8, gw, table, tile):
    n, npair = gw.shape
    slot_rows = (SUBLANES - 1) * SLOT_STRIDE + npair // 2
    slot_rows = -(-slot_rows // SUBLANES) * SUBLANES
    tok = pl.BlockSpec((tile, npair), lambda i: (i, 0))
    return pl.pallas_call(
        functools.partial(_peer_u_kernel, tile=tile),
        grid=(n // tile,),
        in_specs=[pl.BlockSpec(memory_space=pl.ANY),
                  pl.BlockSpec((tile * SUBLANES, LANES), lambda i: (i, 0)),
                  tok,
                  pl.BlockSpec(memory_space=pltpu.VMEM)],
        out_specs=pl.BlockSpec((tile, npair // 2 * SUBLANES), lambda i: (i, 0)),
        out_shape=jax.ShapeDtypeStruct((n, npair // 2 * SUBLANES), F32),
        scratch_shapes=[pltpu.VMEM((slot_rows, LANES), F32),
                        pltpu.VMEM((slot_rows, LANES), F32),
                        pltpu.VMEM((npair, LANES), F32),
                        pltpu.SMEM((tile // 2 * npair,), jnp.int32),
                        pltpu.SMEM((tile // 2 * npair,), jnp.int32),
                        pltpu.SemaphoreType.DMA((2,))],
        compiler_params=pltpu.CompilerParams(dimension_semantics=("arbitrary",),
                                             vmem_limit_bytes=VMEM_LIMIT),
        name="peer_u",
    )(eflat, x8, gw, table)


def _peer_v_kernel(e_hbm, w_ref, tbl_ref, o_ref, wb_a, wb_b, e_scr0, e_scr1, sems, *, tile):
    npair = PEER_HEADS * PEER_TOPK
    half = npair // 2
    sub = tile // 2
    nacc = 4

    def process(s, e_idx):
        first = s * sub

        def spread(k, wb):
            row = w_ref[pl.ds(first + k, 1), :]
            wb[...] = jnp.broadcast_to(row, (LANES, row.shape[1])).T

        def accumulate(k, wb):
            acc_lo = [jnp.zeros((SUBLANES, LANES), F32) for _ in range(nacc)]
            acc_hi = [jnp.zeros((SUBLANES, LANES), F32) for _ in range(nacc)]
            for q in range(half):
                lo, hi = _unpack_pair(_gather_pair_words(tbl_ref, e_idx, k, q, half, sub))
                w = wb[pl.ds(q * SUBLANES, SUBLANES), :]
                acc_lo[q % nacc] = acc_lo[q % nacc] + w * lo
                acc_hi[q % nacc] = acc_hi[q % nacc] + w * hi
            lo = (acc_lo[0] + acc_lo[1]) + (acc_lo[2] + acc_lo[3])
            hi = (acc_hi[0] + acc_hi[1]) + (acc_hi[2] + acc_hi[3])
            lo = lo[0:ROW_SUB] + lo[ROW_SUB:]
            hi = hi[0:ROW_SUB] + hi[ROW_SUB:]
            rows = pl.ds(pl.multiple_of((first + k) * SUBLANES, SUBLANES), SUBLANES)
            o_ref[rows, :] = jnp.concatenate([lo, hi], axis=0)

        spread(0, wb_a)

        wbs = (wb_a, wb_b)

        def token_group(i, carry):
            k0 = PEER_UNROLL * i
            for j in range(PEER_UNROLL):
                nxt = k0 + j + 1 if j + 1 < PEER_UNROLL else jnp.minimum(k0 + PEER_UNROLL, sub - 1)
                spread(nxt, wbs[(j + 1) % 2])
                accumulate(k0 + j, wbs[j % 2])
            return carry

        lax.fori_loop(0, sub // PEER_UNROLL, token_group, 0)

    _with_prefetched_indices(e_hbm, (e_scr0, e_scr1), sems, process)


def _peer_v(eflat, w2, table, tile):
    n, wide = w2.shape
    npair = PEER_HEADS * PEER_TOPK
    return pl.pallas_call(
        functools.partial(_peer_v_kernel, tile=tile),
        grid=(n // tile,),
        in_specs=[pl.BlockSpec(memory_space=pl.ANY),
                  pl.BlockSpec((tile, wide), lambda i: (i, 0)),
                  pl.BlockSpec(memory_space=pltpu.VMEM)],
        out_specs=pl.BlockSpec((tile * SUBLANES, LANES), lambda i: (i, 0)),
        out_shape=jax.ShapeDtypeStruct((n * SUBLANES, LANES), F32),
        scratch_shapes=[pltpu.VMEM((wide, LANES), F32),
                        pltpu.VMEM((wide, LANES), F32),
                        pltpu.SMEM((tile // 2 * npair,), jnp.int32),
                        pltpu.SMEM((tile // 2 * npair,), jnp.int32),
                        pltpu.SemaphoreType.DMA((2,))],
        compiler_params=pltpu.CompilerParams(dimension_semantics=("arbitrary",),
                                             vmem_limit_bytes=VMEM_LIMIT),
        name="peer_v",
    )(eflat, w2, table)


def _pack_table(t):
    bits = lax.bitcast_convert_type(t.astype(BF16), jnp.uint16).astype(jnp.uint32)
    words = bits[:, :ROW_WORDS] | (bits[:, ROW_WORDS:] << 16)
    return lax.bitcast_convert_type(words, jnp.int32).reshape(t.shape[0] * ROW_SUB, LANES)


def _final_kernel(x_ref, p_ref, g2_ref, lg_ref, lb_ref, o_ref, *, alpha):
    tm, D = x_ref.shape[1], x_ref.shape[2]
    chunks = D // LANES
    peer = jnp.concatenate([p_ref[pl.ds(j, tm, stride=chunks), :] for j in range(chunks)], axis=1)
    o_ref[0] = _ln(alpha * x_ref[0] + g2_ref[0] * peer, LN_EPS) * lg_ref[...] + lb_ref[...]


def _final(x1, peer, g2, ln2_g, ln2_b, alpha, tm):
    B, T, D = x1.shape
    tok = pl.BlockSpec((1, tm, D), lambda b, i: (b, i, 0))
    vec = pl.BlockSpec((1, D), lambda b, i: (0, 0))
    return pl.pallas_call(
        functools.partial(_final_kernel, alpha=alpha),
        grid=(B, T // tm),
        in_specs=[tok, pl.BlockSpec((tm * D // LANES, LANES), lambda b, i: (b * (T // tm) + i, 0)),
                  pl.BlockSpec((1, 1, D), lambda b, i: (b, 0, 0)), vec, vec],
        out_specs=tok,
        out_shape=jax.ShapeDtypeStruct((B, T, D), F32),
        compiler_params=pltpu.CompilerParams(dimension_semantics=("parallel", "parallel")),
        name="final",
    )(x1, peer, g2, ln2_g, ln2_b)


def _retention_consts(chunk, t_valid):
    lg = jnp.log1p(-(2.0 ** (-5.0 - jnp.arange(N_HEADS, dtype=F32))))
    pos = jnp.arange(chunk, dtype=F32)
    diff = pos[:, None] - pos[None, :]
    dmat = jnp.where(diff >= 0, jnp.exp(lg[:, None, None] * jnp.maximum(diff, 0.0)), 0.0)
    inter = jnp.exp(lg[:, None] * (pos + 1.0))[..., None]
    wend = jnp.where(pos < t_valid, jnp.exp(lg[:, None] * (t_valid - 1.0 - pos)), 0.0)[..., None]
    decay = jnp.exp(lg * t_valid)[:, None, None]
    return dmat, inter, wend, decay


def _rotary_tables(T, pos0):
    half = DH // 2
    inv = ROPE_BASE ** (-jnp.arange(half, dtype=F32) / half)
    ang = (pos0 + jnp.arange(T, dtype=F32))[:, None] * inv[None, :]
    cos, sin = jnp.cos(ang), jnp.sin(ang)
    return jnp.concatenate([cos, cos], axis=-1), jnp.concatenate([-sin, sin], axis=-1)


def _layer(x, c, pos0, t_valid, C0, n0, m0, S0, p, alpha):
    B, T, D = x.shape
    W = N_HEADS * DH
    mod = _ada(c, p["w_ada"], p["b_ada"])
    sh1, sc1, g1, sh2, sc2, g2 = [m[:, None, :] for m in jnp.split(mod, 6, axis=-1)]

    w_in, b_in = p["w_in"], p["b_in"]
    g_off = 4 * W
    r_off = g_off + 2 * N_HEADS
    wm = w_in[:, :g_off].astype(BF16)
    wr = w_in[:, r_off:].astype(BF16)
    wg = jnp.pad(w_in[:, g_off:r_off], ((0, 0), (0, LANES - 2 * N_HEADS)))
    bg = jnp.pad(b_in[g_off:r_off], (0, LANES - 2 * N_HEADS))[None, :]
    wgt = w_in[:, g_off:r_off].T
    bgt = b_in[g_off:r_off][:, None]
    tm = min(T, 512)
    zm, zr, zg, zgt = _inproj(x, sh1, sc1, wm, b_in[None, :g_off], wr, b_in[None, r_off:],
                              wg, bg, wgt, bgt, tm)

    chunk = min(T, MIX_CHUNK)
    tv = min(t_valid, chunk) if T == chunk else chunk
    cos2, sin2 = _rotary_tables(T, pos0)
    dmat, inter, wend, decay = _retention_consts(chunk, tv)
    bc = lambda a: jnp.broadcast_to(a[..., None, :], a.shape[:-1] + (SUBLANES, a.shape[-1]))
    n0b = bc(n0.astype(F32))
    m0b = jnp.broadcast_to(m0.astype(F32)[..., None, None], m0.shape + (SUBLANES, DH))
    ymix, C1, n1, m1, S1 = _mixers(zm, zr, zg, zgt, cos2, sin2, dmat, inter, wend, decay,
                                   p["head_g"][None, :], C0.astype(F32), n0b, m0b, S0.astype(F32),
                                   chunk, tv)
    n1 = n1[:, :, 0, :]
    m1 = m1[:, :, 0, 0]

    x1, h2, eidx, gw = _post(x, ymix, g1, sh2, sc2, p["w_out"].astype(BF16), p["ln1_g"][None, :],
                             p["ln1_b"][None, :], p["w_pq"].astype(BF16), p["sub_keys"], alpha,
                             POST_TILE)
    npair = PEER_HEADS * PEER_TOPK
    chunks = D // LANES
    tile = PEER_TILE
    sub = tile // 2
    nv = B * t_valid
    n = -(-nv // tile) * tile

    def valid_rows(a, per_token):
        if t_valid == T and n == nv:
            return a
        a = a.reshape(B, T * per_token, a.shape[-1])[:, :t_valid * per_token]
        return jnp.pad(a.reshape(nv * per_token, a.shape[-1]), ((0, (n - nv) * per_token), (0, 0)))

    x8 = valid_rows(h2, chunks)
    e2 = valid_rows(eidx.reshape(B * T, npair), 1)
    gw2 = valid_rows(gw.reshape(B * T, npair), 1)
    eflat = e2.reshape(n // sub, sub, npair).transpose(0, 2, 1).reshape(n * npair)
    wts = _peer_u(eflat, x8, gw2, _pack_table(p["expert_u"]), tile)
    peer = _peer_v(eflat, wts, _pack_table(p["expert_v"]), tile)
    if not (t_valid == T and n == nv):
        peer = peer[:nv * chunks].reshape(B, t_valid * chunks, LANES)
        peer = jnp.pad(peer, ((0, 0), (0, (T - t_valid) * chunks), (0, 0))).reshape(B * T * chunks, LANES)
    y = _final(x1, peer, g2, p["ln2_g"][None, :], p["ln2_b"][None, :], alpha, min(T, 512))
    return y, (C1, n1, m1, S1)


def kernel(x_prompt, x_sample, c_prompt, c_sample, state_mlstm_C, state_mlstm_n, state_mlstm_m,
           state_ret_S, w_ada, b_ada, w_in, b_in, head_g, w_out, ln1_g, ln1_b, w_pq, sub_keys,
           expert_u, expert_v, ln2_g, ln2_b):
    depth = w_ada.shape[0]
    alpha = (2.0 * depth) ** 0.25
    params = dict(w_ada=w_ada, b_ada=b_ada, w_in=w_in, b_in=b_in, head_g=head_g, w_out=w_out,
                  ln1_g=ln1_g, ln1_b=ln1_b, w_pq=w_pq, sub_keys=sub_keys, expert_u=expert_u,
                  expert_v=expert_v, ln2_g=ln2_g, ln2_b=ln2_b)

    def run(x, c, pos0, C0, n0, m0, S0):
        B, T, D = x.shape
        t_valid = T
        if T % LANES:
            x = jnp.pad(x, ((0, 0), (0, LANES - T % LANES), (0, 0)))
        Cs, ns, ms, Ss = [], [], [], []
        for l in range(depth):
            pl_ = {k: v[l] for k, v in params.items()}
            x, (C1, n1, m1, S1) = _layer(x, c, pos0, t_valid, C0[l], n0[l], m0[l], S0[l], pl_, alpha)
            Cs.append(C1); ns.append(n1); ms.append(m1); Ss.append(S1)
        return x[:, :T], jnp.stack(Cs), jnp.stack(ns), jnp.stack(ms), jnp.stack(Ss)

    B = x_prompt.shape[0]
    H = state_mlstm_C.shape[2]
    zC = jnp.zeros((depth, B, H, DH, DH), F32)
    zn = jnp.zeros((depth, B, H, DH), F32)
    zm = jnp.zeros((depth, B, H), F32)
    zS = jnp.zeros((depth, B, H, DH, DH), F32)
    past_len = 1024
    y_p, C_p, n_p, m_p, S_p = run(x_prompt, c_prompt, 0, zC, zn, zm, zS)
    y_s, C_s, n_s, m_s, S_s = run(x_sample, c_sample, past_len, state_mlstm_C, state_mlstm_n,
                                  state_mlstm_m, state_ret_S)
    return (y_p, y_s, C_p, n_p, m_p, S_p, C_s, n_s, m_s, S_s)
```

```python
import functools
import math

import jax
import jax.numpy as jnp
from jax import lax
from jax.experimental import pallas as pl
from jax.experimental.pallas import tpu as pltpu

F32 = jnp.float32
BF16 = jnp.bfloat16
HIGHEST = lax.Precision.HIGHEST

LN_EPS = 1e-5
HEAD_EPS = 1e-6
ROPE_BASE = 10000.0
N_HEADS = 4
DH = 128
LANES = 128
SUBLANES = 8
PEER_HEADS = 8
PEER_TOPK = 16
MIX_CHUNK = 512
POST_TILE = 128
PEER_TILE = 128
PEER_UNROLL = 32
ROW_WORDS = 512
ROW_SUB = ROW_WORDS // LANES
SLOT_STRIDE = 65
MASKED_GATE = -1e30
VMEM_LIMIT = 56 * 1024 * 1024


def _ln(x, eps):
    mu = jnp.mean(x, axis=-1, keepdims=True)
    xc = x - mu
    var = jnp.mean(xc * xc, axis=-1, keepdims=True)
    return xc * lax.rsqrt(var + eps)


def _dot(a, b):
    return jnp.dot(a.astype(BF16), b.astype(BF16), preferred_element_type=F32)


def _dot_nt(a, b):
    return lax.dot_general(a.astype(BF16), b.astype(BF16), (((1,), (1,)), ((), ())),
                           preferred_element_type=F32)


def _dot_tn(a, b):
    return lax.dot_general(a.astype(BF16), b.astype(BF16), (((0,), (0,)), ((), ())),
                           preferred_element_type=F32)


def _log_sigmoid(x):
    return jnp.minimum(x, 0.0) - jnp.log1p(jnp.exp(-jnp.abs(x)))


def _ada_kernel(c_ref, w_ref, b_ref, o_ref):
    c = c_ref[...]
    s = c * jax.nn.sigmoid(c)
    o_ref[...] = jnp.dot(s, w_ref[...], preferred_element_type=F32, precision=HIGHEST) + b_ref[...]


def _ada(c, w_ada, b_ada):
    B, D = c.shape
    n = w_ada.shape[1]
    return pl.pallas_call(
        _ada_kernel,
        grid=(n // D,),
        in_specs=[pl.BlockSpec((B, D), lambda j: (0, 0)),
                  pl.BlockSpec((D, D), lambda j: (0, j)),
                  pl.BlockSpec((1, D), lambda j: (0, j))],
        out_specs=pl.BlockSpec((B, D), lambda j: (0, j)),
        out_shape=jax.ShapeDtypeStruct((B, n), F32),
        name="ada",
    )(c, w_ada, b_ada.reshape(1, n))


def _inproj_kernel(x_ref, sh_ref, sc_ref, wm_ref, bm_ref, wr_ref, br_ref, wg_ref, bg_ref,
                   wgt_ref, bgt_ref, zm_ref, zr_ref, zg_ref, zgt_ref):
    h = _ln(x_ref[0], LN_EPS) * (1.0 + sc_ref[0]) + sh_ref[0]
    hb = h.astype(BF16)
    zm_ref[0] = jnp.dot(hb, wm_ref[...], preferred_element_type=F32) + bm_ref[...]
    zr_ref[0] = jnp.dot(hb, wr_ref[...], preferred_element_type=F32) + br_ref[...]
    zg_ref[0] = jnp.dot(h, wg_ref[...], preferred_element_type=F32, precision=HIGHEST) + bg_ref[...]
    zgt_ref[0] = lax.dot_general(wgt_ref[...], h, (((1,), (1,)), ((), ())),
                                 preferred_element_type=F32, precision=HIGHEST) + bgt_ref[...]


def _inproj(x, sh, sc, wm, bm, wr, br, wg, bg, wgt, bgt, tm):
    B, T, D = x.shape
    wm_n, wr_n = wm.shape[1], wr.shape[1]
    full = lambda a: pl.BlockSpec(a.shape, lambda b, i: (0,) * a.ndim)
    return pl.pallas_call(
        _inproj_kernel,
        grid=(B, T // tm),
        in_specs=[pl.BlockSpec((1, tm, D), lambda b, i: (b, i, 0)),
                  pl.BlockSpec((1, 1, D), lambda b, i: (b, 0, 0)),
                  pl.BlockSpec((1, 1, D), lambda b, i: (b, 0, 0)),
                  full(wm), full(bm), full(wr), full(br), full(wg), full(bg), full(wgt), full(bgt)],
        out_specs=[pl.BlockSpec((1, tm, wm_n), lambda b, i: (b, i, 0)),
                   pl.BlockSpec((1, tm, wr_n), lambda b, i: (b, i, 0)),
                   pl.BlockSpec((1, tm, LANES), lambda b, i: (b, i, 0)),
                   pl.BlockSpec((1, SUBLANES, tm), lambda b, i: (b, 0, i))],
        out_shape=[jax.ShapeDtypeStruct((B, T, wm_n), F32),
                   jax.ShapeDtypeStruct((B, T, wr_n), F32),
                   jax.ShapeDtypeStruct((B, T, LANES), F32),
                   jax.ShapeDtypeStruct((B, SUBLANES, T), F32)],
        compiler_params=pltpu.CompilerParams(dimension_semantics=("parallel", "parallel"),
                                             vmem_limit_bytes=VMEM_LIMIT),
        name="inproj",
    )(x, sh, sc, wm, bm, wr, br, wg, bg, wgt, bgt)


def _head_norm(t):
    return _ln(t, HEAD_EPS)


def _mixers_kernel(zm_ref, zr_ref, zg_ref, zgt_ref, cos_ref, sin_ref, dmat_ref, inter_ref,
                   wend_ref, decay_ref, hg_ref, c0_ref, n0_ref, m0_ref, s0_ref,
                   y_ref, c_ref, n_ref, m_ref, s_ref, *, chunk, t_valid):
    L = chunk
    W = N_HEADS * DH

    @pl.when(pl.program_id(1) == 0)
    def _():
        c_ref[...] = c0_ref[...]
        n_ref[...] = n0_ref[...]
        m_ref[...] = m0_ref[...]
        s_ref[...] = s0_ref[...]

    row = lax.broadcasted_iota(jnp.int32, (L, L), 0)
    col = lax.broadcasted_iota(jnp.int32, (L, L), 1)
    tri = col <= row
    zg = zg_ref[0]
    zgt = zgt_ref[0]
    hg = hg_ref[...]
    if t_valid < L:
        valid_col = lax.broadcasted_iota(jnp.int32, (L, 1), 0) < t_valid
        valid_row = lax.broadcasted_iota(jnp.int32, (1, L), 1) < t_valid
    scale = DH ** -0.5

    for h in range(N_HEADS):
        lo, hi = h * DH, (h + 1) * DH
        q = zm_ref[0, :, lo:hi]
        k = zm_ref[0, :, W + lo:W + hi] * scale
        v = zm_ref[0, :, 2 * W + lo:2 * W + hi]
        og = zm_ref[0, :, 3 * W + lo:3 * W + hi]
        ig_col = zg[:, h:h + 1]
        lf_col = _log_sigmoid(zg[:, N_HEADS + h:N_HEADS + h + 1])
        ig_row = zgt[h:h + 1, :]
        lf_row = _log_sigmoid(zgt[N_HEADS + h:N_HEADS + h + 1, :])
        if t_valid < L:
            ig_col = jnp.where(valid_col, ig_col, MASKED_GATE)
            ig_row = jnp.where(valid_row, ig_row, MASKED_GATE)
            lf_col = jnp.where(valid_col, lf_col, 0.0)
            lf_row = jnp.where(valid_row, lf_row, 0.0)
        C = c_ref[0, h]
        n_row = n_ref[0, h, 0:1, :]
        m_prev = m_ref[0, h, 0:1, 0:1]

        b_col = jnp.sum(jnp.where(tri, lf_row, 0.0), axis=1, keepdims=True)
        b_row = jnp.sum(jnp.where(row <= col, lf_col, 0.0), axis=0, keepdims=True)
        a_col = ig_col - b_col
        a_row = ig_row - b_row
        cm_col = jnp.max(jnp.where(tri, a_row, -jnp.inf), axis=1, keepdims=True)
        M_col = jnp.maximum(m_prev, cm_col)
        dm = jnp.exp(jnp.where(tri, a_row - M_col, -jnp.inf))
        inter = jnp.exp(m_prev - M_col)
        s = _dot_nt(q, k) * dm
        num = inter * _dot(q, C) + _dot(s, v)
        den = inter * jnp.sum(q * n_row, axis=1, keepdims=True) + jnp.sum(s, axis=1, keepdims=True)
        hh = num / jnp.maximum(jnp.abs(den), jnp.exp(-(b_col + M_col)))
        M_end = M_col[L - 1:L, :]
        w_end = jnp.exp(a_col - M_end)
        decay = jnp.exp(m_prev - M_end)
        kw = k * w_end
        c_ref[0, h] = decay * C + _dot_tn(kw, v)
        n_new = decay * n_row + jnp.sum(kw, axis=0, keepdims=True)
        n_ref[0, h] = jnp.broadcast_to(n_new, (SUBLANES, DH))
        m_new = b_col[L - 1:L, :] + M_end
        m_ref[0, h] = jnp.broadcast_to(m_new, (SUBLANES, DH))
        out_a = _head_norm(hh) * jax.nn.sigmoid(og)
        y_ref[0, :, lo:hi] = out_a * hg[:, lo:hi]

        cos2 = cos_ref[...]
        sin2 = sin_ref[...]
        rq = zr_ref[0, :, lo:hi]
        rk = zr_ref[0, :, W + lo:W + hi]
        rv = zr_ref[0, :, 2 * W + lo:2 * W + hi]
        rg = zr_ref[0, :, 3 * W + lo:3 * W + hi]
        qb = rq * cos2 + pltpu.roll(rq, DH // 2, axis=1) * sin2
        kb = (rk * cos2 + pltpu.roll(rk, DH // 2, axis=1) * sin2) * scale
        S = s_ref[0, h]
        sr = _dot_nt(qb, kb) * dmat_ref[h]
        o = _dot(sr, rv) + inter_ref[h] * _dot(qb, S)
        s_ref[0, h] = decay_ref[h] * S + _dot_tn(kb * wend_ref[h], rv)
        out_b = _head_norm(o) * (rg * jax.nn.sigmoid(rg))
        y_ref[0, :, W + lo:W + hi] = out_b * hg[:, W + lo:W + hi]


def _mixers(zm, zr, zg, zgt, cos2, sin2, dmat, inter, wend, decay, head_g, C0, n0, m0, S0,
            chunk, t_valid):
    B, T, _ = zm.shape
    nC = T // chunk
    D = 2 * N_HEADS * DH
    full = lambda a: pl.BlockSpec(a.shape, lambda b, c: (0,) * a.ndim)
    st4 = pl.BlockSpec((1, N_HEADS, DH, DH), lambda b, c: (b, 0, 0, 0))
    st3 = pl.BlockSpec((1, N_HEADS, SUBLANES, DH), lambda b, c: (b, 0, 0, 0))
    return pl.pallas_call(
        functools.partial(_mixers_kernel, chunk=chunk, t_valid=t_valid),
        grid=(B, nC),
        in_specs=[pl.BlockSpec((1, chunk, zm.shape[2]), lambda b, c: (b, c, 0)),
                  pl.BlockSpec((1, chunk, zr.shape[2]), lambda b, c: (b, c, 0)),
                  pl.BlockSpec((1, chunk, LANES), lambda b, c: (b, c, 0)),
                  pl.BlockSpec((1, SUBLANES, chunk), lambda b, c: (b, 0, c)),
                  pl.BlockSpec((chunk, DH), lambda b, c: (c, 0)),
                  pl.BlockSpec((chunk, DH), lambda b, c: (c, 0)),
                  full(dmat), full(inter), full(wend), full(decay), full(head_g),
                  st4, st3, st3, st4],
        out_specs=[pl.BlockSpec((1, chunk, D), lambda b, c: (b, c, 0)), st4, st3, st3, st4],
        out_shape=[jax.ShapeDtypeStruct((B, T, D), F32),
                   jax.ShapeDtypeStruct(C0.shape, F32),
                   jax.ShapeDtypeStruct(n0.shape, F32),
                   jax.ShapeDtypeStruct(m0.shape, F32),
                   jax.ShapeDtypeStruct(S0.shape, F32)],
        compiler_params=pltpu.CompilerParams(dimension_semantics=("parallel", "arbitrary"),
                                             vmem_limit_bytes=VMEM_LIMIT),
        name="mixers",
    )(zm, zr, zg, zgt, cos2, sin2, dmat, inter, wend, decay, head_g, C0, n0, m0, S0)


def _argmax_blocks(vals, tags, extras=()):
    v, t = list(vals), list(tags)
    ex = [list(e) for e in extras]
    while len(v) > 1:
        nv, nt, ne = [], [], [[] for _ in ex]
        for a in range(0, len(v) - 1, 2):
            take = v[a] >= v[a + 1]
            nv.append(jnp.where(take, v[a], v[a + 1]))
            nt.append(jnp.where(take, t[a], t[a + 1]))
            for q, e in enumerate(ex):
                ne[q].append(jnp.where(take, e[a], e[a + 1]))
        if len(v) % 2:
            nv.append(v[-1])
            nt.append(t[-1])
            for q, e in enumerate(ex):
                ne[q].append(e[-1])
        v, t, ex = nv, nt, ne
    m = jnp.max(v[0], axis=0, keepdims=True)
    tag = jnp.min(jnp.where(v[0] == m, t[0], jnp.int32(2 ** 30)), axis=0, keepdims=True)
    outs = [jnp.sum(jnp.where(t[0] == tag, e[0], 0), axis=0, keepdims=True) for e in ex]
    return m, tag, outs


def _topk_sorted(s, k):
    n, tok = s.shape
    nb = n // SUBLANES
    sub = lax.broadcasted_iota(jnp.int32, (SUBLANES, tok), 0)
    blocks = [s[b * SUBLANES:(b + 1) * SUBLANES] for b in range(nb)]
    tags = [sub + b * SUBLANES for b in range(nb)]
    v_rows, i_rows = [], []
    v_blk = [jnp.zeros((SUBLANES, tok), F32) for _ in range(k // SUBLANES)]
    i_blk = [jnp.zeros((SUBLANES, tok), jnp.int32) for _ in range(k // SUBLANES)]
    for r in range(k):
        m, i, _ = _argmax_blocks(blocks, tags)
        v_rows.append(m)
        i_rows.append(i)
        at = sub == (r % SUBLANES)
        v_blk[r // SUBLANES] = jnp.where(at, m, v_blk[r // SUBLANES])
        i_blk[r // SUBLANES] = jnp.where(at, i, i_blk[r // SUBLANES])
        blocks = [jnp.where(t == i, -jnp.inf, b) for b, t in zip(blocks, tags)]
    return v_rows, i_rows, v_blk, i_blk


def _product_topk(top0, top1, nk):
    K = PEER_TOPK
    v0_rows, i0_rows, v0_blk, i0_blk = top0
    v1_rows, i1_rows, v1_blk, i1_blk = top1
    tok = v0_rows[0].shape[1]
    sub = lax.broadcasted_iota(jnp.int32, (SUBLANES, tok), 0)
    vals, flats, experts = [], [], []
    for i in range(SUBLANES):
        for jb in range(K // SUBLANES):
            limit = K // (i + 1) - jb * SUBLANES
            if limit <= 0:
                continue
            blk = v0_rows[i] + v1_blk[jb]
            if limit < SUBLANES:
                blk = jnp.where(sub < limit, blk, -jnp.inf)
            vals.append(blk)
            flats.append(sub + (i * K + jb * SUBLANES))
            experts.append(i0_rows[i] * nk + i1_blk[jb])
    vals.append(v0_blk[1] + v1_rows[0])
    flats.append((sub + SUBLANES) * K)
    experts.append(i0_blk[1] * nk + i1_rows[0])
    tv, te = [], []
    for _ in range(K):
        m, f, (e,) = _argmax_blocks(vals, flats, extras=(experts,))
        tv.append(m)
        te.append(e)
        vals = [jnp.where(t == f, -jnp.inf, b) for b, t in zip(vals, flats)]
    return tv, te


def _post_kernel(x_ref, y_ref, g1_ref, sh2_ref, sc2_ref, wout_ref, l1g_ref, l1b_ref, wpq_ref,
                 keys_ref, x1_ref, h2_ref, e_ref, g_ref, *, alpha):
    mix = _dot(y_ref[0], wout_ref[...])
    x1 = _ln(alpha * x_ref[0] + g1_ref[0] * mix, LN_EPS) * l1g_ref[...] + l1b_ref[...]
    x1_ref[0] = x1
    h2 = _ln(x1, LN_EPS) * (1.0 + sc2_ref[0]) + sh2_ref[0]
    chunks = h2.shape[1] // LANES
    for j in range(chunks):
        h2_ref[pl.ds(j, h2.shape[0], stride=chunks), :] = h2[:, j * LANES:(j + 1) * LANES]
    qall = _dot(h2, wpq_ref[...])
    nk = keys_ref.shape[1]
    K = PEER_TOPK
    e_rows, g_rows = [], []
    for h in range(PEER_HEADS):
        tops = []
        for c in range(2):
            lo = (h * 2 + c) * DH
            st = _dot_nt(keys_ref[c], qall[:, lo:lo + DH])
            tops.append(_topk_sorted(st, K))
        tv, te = _product_topk(tops[0], tops[1], nk)
        tv = jnp.concatenate(tv, axis=0)
        ex = jnp.exp(tv - tv[0:1, :])
        g_rows.append(ex / jnp.sum(ex, axis=0, keepdims=True))
        e_rows.append(jnp.concatenate(te, axis=0))
    e_all = jnp.concatenate(e_rows, axis=0) * ROW_SUB
    g_all = jnp.concatenate(g_rows, axis=0)
    e_ref[0] = pltpu.bitcast(pltpu.bitcast(e_all, F32).T, jnp.int32)
    g_ref[0] = g_all.T


def _post(x, ymix, g1, sh2, sc2, w_out, ln1_g, ln1_b, w_pq, keys, alpha, tm):
    B, T, D = x.shape
    npair = PEER_HEADS * PEER_TOPK
    full = lambda a: pl.BlockSpec(a.shape, lambda b, i: (0,) * a.ndim)
    tok = lambda w: pl.BlockSpec((1, tm, w), lambda b, i: (b, i, 0))
    bat = pl.BlockSpec((1, 1, D), lambda b, i: (b, 0, 0))
    return pl.pallas_call(
        functools.partial(_post_kernel, alpha=alpha),
        grid=(B, T // tm),
        in_specs=[tok(D), tok(D), bat, bat, bat, full(w_out), full(ln1_g), full(ln1_b),
                  full(w_pq), full(keys)],
        out_specs=[tok(D), pl.BlockSpec((tm * D // LANES, LANES), lambda b, i: (b * (T // tm) + i, 0)),
                   tok(npair), tok(npair)],
        out_shape=[jax.ShapeDtypeStruct((B, T, D), F32),
                   jax.ShapeDtypeStruct((B * T * D // LANES, LANES), F32),
                   jax.ShapeDtypeStruct((B, T, npair), jnp.int32),
                   jax.ShapeDtypeStruct((B, T, npair), F32)],
        compiler_params=pltpu.CompilerParams(dimension_semantics=("parallel", "parallel"),
                                             vmem_limit_bytes=VMEM_LIMIT),
        name="post",
    )(x, ymix, g1, sh2, sc2, w_out, ln1_g, ln1_b, w_pq, keys)


def _unpack_pair(words):
    lo = pltpu.bitcast(words << 16, F32)
    hi = pltpu.bitcast(words & jnp.int32(-65536), F32)
    return lo, hi


def _gelu_exact(x):
    return 0.5 * x * (1.0 + lax.erf(x * (1.0 / math.sqrt(2.0))))


def _index_copy(e_hbm, e_scrs, sems, step, s):
    words = e_scrs[s].shape[0]
    return pltpu.make_async_copy(e_hbm.at[pl.ds((step * 2 + s) * words, words)], e_scrs[s], sems.at[s])


def _with_prefetched_indices(e_hbm, e_scrs, sems, process):
    step = pl.program_id(0)

    @pl.when(step == 0)
    def _():
        _index_copy(e_hbm, e_scrs, sems, step, 0).start()

    _index_copy(e_hbm, e_scrs, sems, step, 1).start()
    _index_copy(e_hbm, e_scrs, sems, step, 0).wait()
    process(0, e_scrs[0])
    _index_copy(e_hbm, e_scrs, sems, step, 1).wait()

    @pl.when(step + 1 < pl.num_programs(0))
    def _():
        _index_copy(e_hbm, e_scrs, sems, step + 1, 0).start()

    process(1, e_scrs[1])


def _gather_pair_words(tbl_ref, e_idx, k, q, half, sub):
    ra = pl.multiple_of(e_idx.at[pl.ds(q * sub, sub)][k], ROW_SUB)
    rb = pl.multiple_of(e_idx.at[pl.ds((half + q) * sub, sub)][k], ROW_SUB)
    return jnp.concatenate([tbl_ref[pl.ds(ra, ROW_SUB), :], tbl_ref[pl.ds(rb, ROW_SUB), :]], axis=0)


def _peer_u_kernel(e_hbm, x_ref, g_ref, tbl_ref, w_ref, slot_a, slot_b, act_ref, e_scr0, e_scr1, sems,
                   *, tile):
    npair = PEER_HEADS * PEER_TOPK
    half = npair // 2
    sub = tile // 2
    lane = lax.broadcasted_iota(jnp.int32, (npair, LANES), 1)

    def process(s, e_idx):
        first = s * sub

        def gather(k, slot):
            t = first + k
            x8 = x_ref[pl.ds(pl.multiple_of(t * SUBLANES, SUBLANES), SUBLANES), :]
            xlo = jnp.concatenate([x8[0:ROW_SUB], x8[0:ROW_SUB]], axis=0)
            xhi = jnp.concatenate([x8[ROW_SUB:], x8[ROW_SUB:]], axis=0)
            for q in range(half):
                lo, hi = _unpack_pair(_gather_pair_words(tbl_ref, e_idx, k, q, half, sub))
                slot[pl.ds(q, SUBLANES, stride=SLOT_STRIDE), :] = lo * xlo + hi * xhi

        def reduce(k, slot):
            parts = []
            for g in range(2):
                part = slot[pl.ds(g * ROW_SUB * SLOT_STRIDE, half), :]
                for j in range(1, ROW_SUB):
                    part = part + slot[pl.ds((g * ROW_SUB + j) * SLOT_STRIDE, half), :]
                parts.append(part)
            part = jnp.concatenate(parts, axis=0)
            act = jnp.sum(part, axis=1, keepdims=True)
            act_ref[...] = jnp.where(lane == first + k, act, act_ref[...])

        gather(0, slot_a)

        slots = (slot_a, slot_b)

        def token_group(i, carry):
            k0 = PEER_UNROLL * i
            for j in range(PEER_UNROLL):
                nxt = k0 + j + 1 if j + 1 < PEER_UNROLL else jnp.minimum(k0 + PEER_UNROLL, sub - 1)
                gather(nxt, slots[(j + 1) % 2])
                reduce(k0 + j, slots[j % 2])
            return carry

        lax.fori_loop(0, sub // PEER_UNROLL, token_group, 0)

    act_ref[...] = jnp.zeros_like(act_ref)
    _with_prefetched_indices(e_hbm, (e_scr0, e_scr1), sems, process)
    act_ref[...] = g_ref[...].T * _gelu_exact(act_ref[...])
    top = lax.broadcasted_iota(jnp.int32, (SUBLANES, LANES), 0) < ROW_SUB
    wide = jnp.concatenate([jnp.where(top, act_ref[pl.ds(q, 1), :], act_ref[pl.ds(half + q, 1), :])
                            for q in range(half)], axis=0)
    w_ref[...] = wide.T


def _peer_u(eflat, x8, gw, table, tile):
    n, npair = gw.shape
    slot_rows = (SUBLANES - 1) * SLOT_STRIDE + npair // 2
    slot_rows = -(-slot_rows // SUBLANES) * SUBLANES
    tok = pl.BlockSpec((tile, npair), lambda i: (i, 0))
    return pl.pallas_call(
        functools.partial(_peer_u_kernel, tile=tile),
        grid=(n // tile,),
        in_specs=[pl.BlockSpec(memory_space=pl.ANY),
                  pl.BlockSpec((tile * SUBLANES, LANES), lambda i: (i, 0)),
                  tok,
                  pl.BlockSpec(memory_space=pltpu.VMEM)],
        out_specs=pl.BlockSpec((tile, npair // 2 * SUBLANES), lambda i: (i, 0)),
        out_shape=jax.ShapeDtypeStruct((n, npair // 2 * SUBLANES), F32),
        scratch_shapes=[pltpu.VMEM((slot_rows, LANES), F32),
                        pltpu.VMEM((slot_rows, LANES), F32),
                        pltpu.VMEM((npair, LANES), F32),
                        pltpu.SMEM((tile // 2 * npair,), jnp.int32),
                        pltpu.SMEM((tile // 2 * npair,), jnp.int32),
                        pltpu.SemaphoreType.DMA((2,))],
        compiler_params=pltpu.CompilerParams(dimension_semantics=("arbitrary",),
                                             vmem_limit_bytes=VMEM_LIMIT),
        name="peer_u",
    )(eflat, x8, gw, table)


def _peer_v_kernel(e_hbm, w_ref, tbl_ref, o_ref, wb_a, wb_b, e_scr0, e_scr1, sems, *, tile):
    npair = PEER_HEADS * PEER_TOPK
    half = npair // 2
    sub = tile // 2
    nacc = 4

    def process(s, e_idx):
        first = s * sub

        def spread(k, wb):
            row = w_ref[pl.ds(first + k, 1), :]
            wb[...] = jnp.broadcast_to(row, (LANES, row.shape[1])).T

        def accumulate(k, wb):
            acc_lo = [jnp.zeros((SUBLANES, LANES), F32) for _ in range(nacc)]
            acc_hi = [jnp.zeros((SUBLANES, LANES), F32) for _ in range(nacc)]
            for q in range(half):
                lo, hi = _unpack_pair(_gather_pair_words(tbl_ref, e_idx, k, q, half, sub))
                w = wb[pl.ds(q * SUBLANES, SUBLANES), :]
                acc_lo[q % nacc] = acc_lo[q % nacc] + w * lo
                acc_hi[q % nacc] = acc_hi[q % nacc] + w * hi
            lo = (acc_lo[0] + acc_lo[1]) + (acc_lo[2] + acc_lo[3])
            hi = (acc_hi[0] + acc_hi[1]) + (acc_hi[2] + acc_hi[3])
            lo = lo[0:ROW_SUB] + lo[ROW_SUB:]
            hi = hi[0:ROW_SUB] + hi[ROW_SUB:]
            rows = pl.ds(pl.multiple_of((first + k) * SUBLANES, SUBLANES), SUBLANES)
            o_ref[rows, :] = jnp.concatenate([lo, hi], axis=0)

        spread(0, wb_a)

        wbs = (wb_a, wb_b)

        def token_group(i, carry):
            k0 = PEER_UNROLL * i
            for j in range(PEER_UNROLL):
                nxt = k0 + j + 1 if j + 1 < PEER_UNROLL else jnp.minimum(k0 + PEER_UNROLL, sub - 1)
                spread(nxt, wbs[(j + 1) % 2])
                accumulate(k0 + j, wbs[j % 2])
            return carry

        lax.fori_loop(0, sub // PEER_UNROLL, token_group, 0)

    _with_prefetched_indices(e_hbm, (e_scr0, e_scr1), sems, process)


def _peer_v(eflat, w2, table, tile):
    n, wide = w2.shape
    npair = PEER_HEADS * PEER_TOPK
    return pl.pallas_call(
        functools.partial(_peer_v_kernel, tile=tile),
        grid=(n // tile,),
        in_specs=[pl.BlockSpec(memory_space=pl.ANY),
                  pl.BlockSpec((tile, wide), lambda i: (i, 0)),
                  pl.BlockSpec(memory_space=pltpu.VMEM)],
        out_specs=pl.BlockSpec((tile * SUBLANES, LANES), lambda i: (i, 0)),
        out_shape=jax.ShapeDtypeStruct((n * SUBLANES, LANES), F32),
        scratch_shapes=[pltpu.VMEM((wide, LANES), F32),
                        pltpu.VMEM((wide, LANES), F32),
                        pltpu.SMEM((tile // 2 * npair,), jnp.int32),
                        pltpu.SMEM((tile // 2 * npair,), jnp.int32),
                        pltpu.SemaphoreType.DMA((2,))],
        compiler_params=pltpu.CompilerParams(dimension_semantics=("arbitrary",),
                                             vmem_limit_bytes=VMEM_LIMIT),
        name="peer_v",
    )(eflat, w2, table)


def _pack_table(t):
    bits = lax.bitcast_convert_type(t.astype(BF16), jnp.uint16).astype(jnp.uint32)
    words = bits[:, :ROW_WORDS] | (bits[:, ROW_WORDS:] << 16)
    return lax.bitcast_convert_type(words, jnp.int32).reshape(t.shape[0] * ROW_SUB, LANES)


def _final_kernel(x_ref, p_ref, g2_ref, lg_ref, lb_ref, o_ref, *, alpha):
    tm, D = x_ref.shape[1], x_ref.shape[2]
    chunks = D // LANES
    peer = jnp.concatenate([p_ref[pl.ds(j, tm, stride=chunks), :] for j in range(chunks)], axis=1)
    o_ref[0] = _ln(alpha * x_ref[0] + g2_ref[0] * peer, LN_EPS) * lg_ref[...] + lb_ref[...]


def _final(x1, peer, g2, ln2_g, ln2_b, alpha, tm):
    B, T, D = x1.shape
    tok = pl.BlockSpec((1, tm, D), lambda b, i: (b, i, 0))
    vec = pl.BlockSpec((1, D), lambda b, i: (0, 0))
    return pl.pallas_call(
        functools.partial(_final_kernel, alpha=alpha),
        grid=(B, T // tm),
        in_specs=[tok, pl.BlockSpec((tm * D // LANES, LANES), lambda b, i: (b * (T // tm) + i, 0)),
                  pl.BlockSpec((1, 1, D), lambda b, i: (b, 0, 0)), vec, vec],
        out_specs=tok,
        out_shape=jax.ShapeDtypeStruct((B, T, D), F32),
        compiler_params=pltpu.CompilerParams(dimension_semantics=("parallel", "parallel")),
        name="final",
    )(x1, peer, g2, ln2_g, ln2_b)


def _retention_consts(chunk, t_valid):
    lg = jnp.log1p(-(2.0 ** (-5.0 - jnp.arange(N_HEADS, dtype=F32))))
    pos = jnp.arange(chunk, dtype=F32)
    diff = pos[:, None] - pos[None, :]
    dmat = jnp.where(diff >= 0, jnp.exp(lg[:, None, None] * jnp.maximum(diff, 0.0)), 0.0)
    inter = jnp.exp(lg[:, None] * (pos + 1.0))[..., None]
    wend = jnp.where(pos < t_valid, jnp.exp(lg[:, None] * (t_valid - 1.0 - pos)), 0.0)[..., None]
    decay = jnp.exp(lg * t_valid)[:, None, None]
    return dmat, inter, wend, decay


def _rotary_tables(T, pos0):
    half = DH // 2
    inv = ROPE_BASE ** (-jnp.arange(half, dtype=F32) / half)
    ang = (pos0 + jnp.arange(T, dtype=F32))[:, None] * inv[None, :]
    cos, sin = jnp.cos(ang), jnp.sin(ang)
    return jnp.concatenate([cos, cos], axis=-1), jnp.concatenate([-sin, sin], axis=-1)


def _layer(x, c, pos0, t_valid, C0, n0, m0, S0, p, alpha):
    B, T, D = x.shape
    W = N_HEADS * DH
    mod = _ada(c, p["w_ada"], p["b_ada"])
    sh1, sc1, g1, sh2, sc2, g2 = [m[:, None, :] for m in jnp.split(mod, 6, axis=-1)]

    w_in, b_in = p["w_in"], p["b_in"]
    g_off = 4 * W
    r_off = g_off + 2 * N_HEADS
    wm = w_in[:, :g_off].astype(BF16)
    wr = w_in[:, r_off:].astype(BF16)
    wg = jnp.pad(w_in[:, g_off:r_off], ((0, 0), (0, LANES - 2 * N_HEADS)))
    bg = jnp.pad(b_in[g_off:r_off], (0, LANES - 2 * N_HEADS))[None, :]
    wgt = w_in[:, g_off:r_off].T
    bgt = b_in[g_off:r_off][:, None]
    tm = min(T, 512)
    zm, zr, zg, zgt = _inproj(x, sh1, sc1, wm, b_in[None, :g_off], wr, b_in[None, r_off:],
                              wg, bg, wgt, bgt, tm)

    chunk = min(T, MIX_CHUNK)
    tv = min(t_valid, chunk) if T == chunk else chunk
    cos2, sin2 = _rotary_tables(T, pos0)
    dmat, inter, wend, decay = _retention_consts(chunk, tv)
    bc = lambda a: jnp.broadcast_to(a[..., None, :], a.shape[:-1] + (SUBLANES, a.shape[-1]))
    n0b = bc(n0.astype(F32))
    m0b = jnp.broadcast_to(m0.astype(F32)[..., None, None], m0.shape + (SUBLANES, DH))
    ymix, C1, n1, m1, S1 = _mixers(zm, zr, zg, zgt, cos2, sin2, dmat, inter, wend, decay,
                                   p["head_g"][None, :], C0.astype(F32), n0b, m0b, S0.astype(F32),
                                   chunk, tv)
    n1 = n1[:, :, 0, :]
    m1 = m1[:, :, 0, 0]

    x1, h2, eidx, gw = _post(x, ymix, g1, sh2, sc2, p["w_out"].astype(BF16), p["ln1_g"][None, :],
                             p["ln1_b"][None, :], p["w_pq"].astype(BF16), p["sub_keys"], alpha,
                             POST_TILE)
    npair = PEER_HEADS * PEER_TOPK
    chunks = D // LANES
    tile = PEER_TILE
    sub = tile // 2
    nv = B * t_valid
    n = -(-nv // tile) * tile

    def valid_rows(a, per_token):
        if t_valid == T and n == nv:
            return a
        a = a.reshape(B, T * per_token, a.shape[-1])[:, :t_valid * per_token]
        return jnp.pad(a.reshape(nv * per_token, a.shape[-1]), ((0, (n - nv) * per_token), (0, 0)))

    x8 = valid_rows(h2, chunks)
    e2 = valid_rows(eidx.reshape(B * T, npair), 1)
    gw2 = valid_rows(gw.reshape(B * T, npair), 1)
    eflat = e2.reshape(n // sub, sub, npair).transpose(0, 2, 1).reshape(n * npair)
    wts = _peer_u(eflat, x8, gw2, _pack_table(p["expert_u"]), tile)
    peer = _peer_v(eflat, wts, _pack_table(p["expert_v"]), tile)
    if not (t_valid == T and n == nv):
        peer = peer[:nv * chunks].reshape(B, t_valid * chunks, LANES)
        peer = jnp.pad(peer, ((0, 0), (0, (T - t_valid) * chunks), (0, 0))).reshape(B * T * chunks, LANES)
    y = _final(x1, peer, g2, p["ln2_g"][None, :], p["ln2_b"][None, :], alpha, min(T, 512))
    return y, (C1, n1, m1, S1)


def kernel(x_prompt, x_sample, c_prompt, c_sample, state_mlstm_C, state_mlstm_n, state_mlstm_m,
           state_ret_S, w_ada, b_ada, w_in, b_in, head_g, w_out, ln1_g, ln1_b, w_pq, sub_keys,
           expert_u, expert_v, ln2_g, ln2_b):
    depth = w_ada.shape[0]
    alpha = (2.0 * depth) ** 0.25
    params = dict(w_ada=w_ada, b_ada=b_ada, w_in=w_in, b_in=b_in, head_g=head_g, w_out=w_out,
                  ln1_g=ln1_g, ln1_b=ln1_b, w_pq=w_pq, sub_keys=sub_keys, expert_u=expert_u,
                  expert_v=expert_v, ln2_g=ln2_g, ln2_b=ln2_b)

    def run(x, c, pos0, C0, n0, m0, S0):
        B, T, D = x.shape
        t_valid = T
        if T % LANES:
            x = jnp.pad(x, ((0, 0), (0, LANES - T % LANES), (0, 0)))
        Cs, ns, ms, Ss = [], [], [], []
        for l in range(depth):
            pl_ = {k: v[l] for k, v in params.items()}
            x, (C1, n1, m1, S1) = _layer(x, c, pos0, t_valid, C0[l], n0[l], m0[l], S0[l], pl_, alpha)
            Cs.append(C1); ns.append(n1); ms.append(m1); Ss.append(S1)
        return x[:, :T], jnp.stack(Cs), jnp.stack(ns), jnp.stack(ms), jnp.stack(Ss)

    B = x_prompt.shape[0]
    H = state_mlstm_C.shape[2]
    zC = jnp.zeros((depth, B, H, DH, DH), F32)
    zn = jnp.zeros((depth, B, H, DH), F32)
    zm = jnp.zeros((depth, B, H), F32)
    zS = jnp.zeros((depth, B, H, DH, DH), F32)
    past_len = 1024
    y_p, C_p, n_p, m_p, S_p = run(x_prompt, c_prompt, 0, zC, zn, zm, zS)
    y_s, C_s, n_s, m_s, S_s = run(x_sample, c_sample, past_len, state_mlstm_C, state_mlstm_n,
                                  state_mlstm_m, state_ret_S)
    return (y_p, y_s, C_p, n_p, m_p, S_p, C_s, n_s, m_s, S_s)
```

```python
import functools
import math

import jax
import jax.numpy as jnp
from jax import lax
from jax.experimental import pallas as pl
from jax.experimental.pallas import tpu as pltpu

F32 = jnp.float32
BF16 = jnp.bfloat16
HIGHEST = lax.Precision.HIGHEST

LN_EPS = 1e-5
HEAD_EPS = 1e-6
ROPE_BASE = 10000.0
N_HEADS = 4
DH = 128
LANES = 128
SUBLANES = 8
PEER_HEADS = 8
PEER_TOPK = 16
MIX_CHUNK = 512
POST_TILE = 256
PEER_TILE = 128
PEER_UNROLL = 32
ROW_WORDS = 512
ROW_SUB = ROW_WORDS // LANES
SLOT_STRIDE = 65
MASKED_GATE = -1e30
VMEM_LIMIT = 56 * 1024 * 1024


def _ln(x, eps):
    mu = jnp.mean(x, axis=-1, keepdims=True)
    xc = x - mu
    var = jnp.mean(xc * xc, axis=-1, keepdims=True)
    return xc * lax.rsqrt(var + eps)


def _dot(a, b):
    return jnp.dot(a.astype(BF16), b.astype(BF16), preferred_element_type=F32)


def _dot_nt(a, b):
    return lax.dot_general(a.astype(BF16), b.astype(BF16), (((1,), (1,)), ((), ())),
                           preferred_element_type=F32)


def _dot_tn(a, b):
    return lax.dot_general(a.astype(BF16), b.astype(BF16), (((0,), (0,)), ((), ())),
                           preferred_element_type=F32)


def _log_sigmoid(x):
    return jnp.minimum(x, 0.0) - jnp.log1p(jnp.exp(-jnp.abs(x)))


def _ada_kernel(c_ref, w_ref, b_ref, o_ref):
    c = c_ref[...]
    s = c * jax.nn.sigmoid(c)
    o_ref[...] = jnp.dot(s, w_ref[...], preferred_element_type=F32, precision=HIGHEST) + b_ref[...]


def _ada(c, w_ada, b_ada):
    B, D = c.shape
    n = w_ada.shape[1]
    return pl.pallas_call(
        _ada_kernel,
        grid=(n // D,),
        in_specs=[pl.BlockSpec((B, D), lambda j: (0, 0)),
                  pl.BlockSpec((D, D), lambda j: (0, j)),
                  pl.BlockSpec((1, D), lambda j: (0, j))],
        out_specs=pl.BlockSpec((B, D), lambda j: (0, j)),
        out_shape=jax.ShapeDtypeStruct((B, n), F32),
        name="ada",
    )(c, w_ada, b_ada.reshape(1, n))


def _inproj_kernel(x_ref, sh_ref, sc_ref, wm_ref, bm_ref, wr_ref, br_ref, wg_ref, bg_ref,
                   wgt_ref, bgt_ref, zm_ref, zr_ref, zg_ref, zgt_ref):
    h = _ln(x_ref[0], LN_EPS) * (1.0 + sc_ref[0]) + sh_ref[0]
    hb = h.astype(BF16)
    zm_ref[0] = jnp.dot(hb, wm_ref[...], preferred_element_type=F32) + bm_ref[...]
    zr_ref[0] = jnp.dot(hb, wr_ref[...], preferred_element_type=F32) + br_ref[...]
    zg_ref[0] = jnp.dot(h, wg_ref[...], preferred_element_type=F32, precision=HIGHEST) + bg_ref[...]
    zgt_ref[0] = lax.dot_general(wgt_ref[...], h, (((1,), (1,)), ((), ())),
                                 preferred_element_type=F32, precision=HIGHEST) + bgt_ref[...]


def _inproj(x, sh, sc, wm, bm, wr, br, wg, bg, wgt, bgt, tm):
    B, T, D = x.shape
    wm_n, wr_n = wm.shape[1], wr.shape[1]
    full = lambda a: pl.BlockSpec(a.shape, lambda b, i: (0,) * a.ndim)
    return pl.pallas_call(
        _inproj_kernel,
        grid=(B, T // tm),
        in_specs=[pl.BlockSpec((1, tm, D), lambda b, i: (b, i, 0)),
                  pl.BlockSpec((1, 1, D), lambda b, i: (b, 0, 0)),
                  pl.BlockSpec((1, 1, D), lambda b, i: (b, 0, 0)),
                  full(wm), full(bm), full(wr), full(br), full(wg), full(bg), full(wgt), full(bgt)],
        out_specs=[pl.BlockSpec((1, tm, wm_n), lambda b, i: (b, i, 0)),
                   pl.BlockSpec((1, tm, wr_n), lambda b, i: (b, i, 0)),
                   pl.BlockSpec((1, tm, LANES), lambda b, i: (b, i, 0)),
                   pl.BlockSpec((1, SUBLANES, tm), lambda b, i: (b, 0, i))],
        out_shape=[jax.ShapeDtypeStruct((B, T, wm_n), F32),
                   jax.ShapeDtypeStruct((B, T, wr_n), F32),
                   jax.ShapeDtypeStruct((B, T, LANES), F32),
                   jax.ShapeDtypeStruct((B, SUBLANES, T), F32)],
        compiler_params=pltpu.CompilerParams(dimension_semantics=("parallel", "parallel"),
                                             vmem_limit_bytes=VMEM_LIMIT),
        name="inproj",
    )(x, sh, sc, wm, bm, wr, br, wg, bg, wgt, bgt)


def _head_norm(t):
    return _ln(t, HEAD_EPS)


def _mixers_kernel(zm_ref, zr_ref, zg_ref, zgt_ref, cos_ref, sin_ref, dmat_ref, inter_ref,
                   wend_ref, decay_ref, hg_ref, c0_ref, n0_ref, m0_ref, s0_ref,
                   y_ref, c_ref, n_ref, m_ref, s_ref, *, chunk, t_valid):
    L = chunk
    W = N_HEADS * DH

    @pl.when(pl.program_id(1) == 0)
    def _():
        c_ref[...] = c0_ref[...]
        n_ref[...] = n0_ref[...]
        m_ref[...] = m0_ref[...]
        s_ref[...] = s0_ref[...]

    row = lax.broadcasted_iota(jnp.int32, (L, L), 0)
    col = lax.broadcasted_iota(jnp.int32, (L, L), 1)
    tri = col <= row
    zg = zg_ref[0]
    zgt = zgt_ref[0]
    hg = hg_ref[...]
    if t_valid < L:
        valid_col = lax.broadcasted_iota(jnp.int32, (L, 1), 0) < t_valid
        valid_row = lax.broadcasted_iota(jnp.int32, (1, L), 1) < t_valid
    scale = DH ** -0.5

    for h in range(N_HEADS):
        lo, hi = h * DH, (h + 1) * DH
        q = zm_ref[0, :, lo:hi]
        k = zm_ref[0, :, W + lo:W + hi] * scale
        v = zm_ref[0, :, 2 * W + lo:2 * W + hi]
        og = zm_ref[0, :, 3 * W + lo:3 * W + hi]
        ig_col = zg[:, h:h + 1]
        lf_col = _log_sigmoid(zg[:, N_HEADS + h:N_HEADS + h + 1])
        ig_row = zgt[h:h + 1, :]
        lf_row = _log_sigmoid(zgt[N_HEADS + h:N_HEADS + h + 1, :])
        if t_valid < L:
            ig_col = jnp.where(valid_col, ig_col, MASKED_GATE)
            ig_row = jnp.where(valid_row, ig_row, MASKED_GATE)
            lf_col = jnp.where(valid_col, lf_col, 0.0)
            lf_row = jnp.where(valid_row, lf_row, 0.0)
        C = c_ref[0, h]
        n_row = n_ref[0, h, 0:1, :]
        m_prev = m_ref[0, h, 0:1, 0:1]

        b_col = jnp.sum(jnp.where(tri, lf_row, 0.0), axis=1, keepdims=True)
        b_row = jnp.sum(jnp.where(row <= col, lf_col, 0.0), axis=0, keepdims=True)
        a_col = ig_col - b_col
        a_row = ig_row - b_row
        cm_col = jnp.max(jnp.where(tri, a_row, -jnp.inf), axis=1, keepdims=True)
        M_col = jnp.maximum(m_prev, cm_col)
        dm = jnp.exp(jnp.where(tri, a_row - M_col, -jnp.inf))
        inter = jnp.exp(m_prev - M_col)
        s = _dot_nt(q, k) * dm
        num = inter * _dot(q, C) + _dot(s, v)
        den = inter * jnp.sum(q * n_row, axis=1, keepdims=True) + jnp.sum(s, axis=1, keepdims=True)
        hh = num / jnp.maximum(jnp.abs(den), jnp.exp(-(b_col + M_col)))
        M_end = M_col[L - 1:L, :]
        w_end = jnp.exp(a_col - M_end)
        decay = jnp.exp(m_prev - M_end)
        kw = k * w_end
        c_ref[0, h] = decay * C + _dot_tn(kw, v)
        n_new = decay * n_row + jnp.sum(kw, axis=0, keepdims=True)
        n_ref[0, h] = jnp.broadcast_to(n_new, (SUBLANES, DH))
        m_new = b_col[L - 1:L, :] + M_end
        m_ref[0, h] = jnp.broadcast_to(m_new, (SUBLANES, DH))
        out_a = _head_norm(hh) * jax.nn.sigmoid(og)
        y_ref[0, :, lo:hi] = out_a * hg[:, lo:hi]

        cos2 = cos_ref[...]
        sin2 = sin_ref[...]
        rq = zr_ref[0, :, lo:hi]
        rk = zr_ref[0, :, W + lo:W + hi]
        rv = zr_ref[0, :, 2 * W + lo:2 * W + hi]
        rg = zr_ref[0, :, 3 * W + lo:3 * W + hi]
        qb = rq * cos2 + pltpu.roll(rq, DH // 2, axis=1) * sin2
        kb = (rk * cos2 + pltpu.roll(rk, DH // 2, axis=1) * sin2) * scale
        S = s_ref[0, h]
        sr = _dot_nt(qb, kb) * dmat_ref[h]
        o = _dot(sr, rv) + inter_ref[h] * _dot(qb, S)
        s_ref[0, h] = decay_ref[h] * S + _dot_tn(kb * wend_ref[h], rv)
        out_b = _head_norm(o) * (rg * jax.nn.sigmoid(rg))
        y_ref[0, :, W + lo:W + hi] = out_b * hg[:, W + lo:W + hi]


def _mixers(zm, zr, zg, zgt, cos2, sin2, dmat, inter, wend, decay, head_g, C0, n0, m0, S0,
            chunk, t_valid):
    B, T, _ = zm.shape
    nC = T // chunk
    D = 2 * N_HEADS * DH
    full = lambda a: pl.BlockSpec(a.shape, lambda b, c: (0,) * a.ndim)
    st4 = pl.BlockSpec((1, N_HEADS, DH, DH), lambda b, c: (b, 0, 0, 0))
    st3 = pl.BlockSpec((1, N_HEADS, SUBLANES, DH), lambda b, c: (b, 0, 0, 0))
    return pl.pallas_call(
        functools.partial(_mixers_kernel, chunk=chunk, t_valid=t_valid),
        grid=(B, nC),
        in_specs=[pl.BlockSpec((1, chunk, zm.shape[2]), lambda b, c: (b, c, 0)),
                  pl.BlockSpec((1, chunk, zr.shape[2]), lambda b, c: (b, c, 0)),
                  pl.BlockSpec((1, chunk, LANES), lambda b, c: (b, c, 0)),
                  pl.BlockSpec((1, SUBLANES, chunk), lambda b, c: (b, 0, c)),
                  pl.BlockSpec((chunk, DH), lambda b, c: (c, 0)),
                  pl.BlockSpec((chunk, DH), lambda b, c: (c, 0)),
                  full(dmat), full(inter), full(wend), full(decay), full(head_g),
                  st4, st3, st3, st4],
        out_specs=[pl.BlockSpec((1, chunk, D), lambda b, c: (b, c, 0)), st4, st3, st3, st4],
        out_shape=[jax.ShapeDtypeStruct((B, T, D), F32),
                   jax.ShapeDtypeStruct(C0.shape, F32),
                   jax.ShapeDtypeStruct(n0.shape, F32),
                   jax.ShapeDtypeStruct(m0.shape, F32),
                   jax.ShapeDtypeStruct(S0.shape, F32)],
        compiler_params=pltpu.CompilerParams(dimension_semantics=("parallel", "arbitrary"),
                                             vmem_limit_bytes=VMEM_LIMIT),
        name="mixers",
    )(zm, zr, zg, zgt, cos2, sin2, dmat, inter, wend, decay, head_g, C0, n0, m0, S0)


def _argmax_blocks(vals, tags, extras=()):
    v, t = list(vals), list(tags)
    ex = [list(e) for e in extras]
    while len(v) > 1:
        nv, nt, ne = [], [], [[] for _ in ex]
        for a in range(0, len(v) - 1, 2):
            take = v[a] >= v[a + 1]
            nv.append(jnp.where(take, v[a], v[a + 1]))
            nt.append(jnp.where(take, t[a], t[a + 1]))
            for q, e in enumerate(ex):
                ne[q].append(jnp.where(take, e[a], e[a + 1]))
        if len(v) % 2:
            nv.append(v[-1])
            nt.append(t[-1])
            for q, e in enumerate(ex):
                ne[q].append(e[-1])
        v, t, ex = nv, nt, ne
    m = jnp.max(v[0], axis=0, keepdims=True)
    tag = jnp.min(jnp.where(v[0] == m, t[0], jnp.int32(2 ** 30)), axis=0, keepdims=True)
    outs = [jnp.sum(jnp.where(t[0] == tag, e[0], 0), axis=0, keepdims=True) for e in ex]
    return m, tag, outs


def _topk_sorted(s, k):
    n, tok = s.shape
    nb = n // SUBLANES
    sub = lax.broadcasted_iota(jnp.int32, (SUBLANES, tok), 0)
    blocks = [s[b * SUBLANES:(b + 1) * SUBLANES] for b in range(nb)]
    tags = [sub + b * SUBLANES for b in range(nb)]
    v_rows, i_rows = [], []
    v_blk = [jnp.zeros((SUBLANES, tok), F32) for _ in range(k // SUBLANES)]
    i_blk = [jnp.zeros((SUBLANES, tok), jnp.int32) for _ in range(k // SUBLANES)]
    for r in range(k):
        m, i, _ = _argmax_blocks(blocks, tags)
        v_rows.append(m)
        i_rows.append(i)
        at = sub == (r % SUBLANES)
        v_blk[r // SUBLANES] = jnp.where(at, m, v_blk[r // SUBLANES])
        i_blk[r // SUBLANES] = jnp.where(at, i, i_blk[r // SUBLANES])
        blocks = [jnp.where(t == i, -jnp.inf, b) for b, t in zip(blocks, tags)]
    return v_rows, i_rows, v_blk, i_blk


def _product_topk(top0, top1, nk):
    K = PEER_TOPK
    v0_rows, i0_rows, v0_blk, i0_blk = top0
    v1_rows, i1_rows, v1_blk, i1_blk = top1
    tok = v0_rows[0].shape[1]
    sub = lax.broadcasted_iota(jnp.int32, (SUBLANES, tok), 0)
    vals, flats, experts = [], [], []
    for i in range(SUBLANES):
        for jb in range(K // SUBLANES):
            limit = K // (i + 1) - jb * SUBLANES
            if limit <= 0:
                continue
            blk = v0_rows[i] + v1_blk[jb]
            if limit < SUBLANES:
                blk = jnp.where(sub < limit, blk, -jnp.inf)
            vals.append(blk)
            flats.append(sub + (i * K + jb * SUBLANES))
            experts.append(i0_rows[i] * nk + i1_blk[jb])
    vals.append(v0_blk[1] + v1_rows[0])
    flats.append((sub + SUBLANES) * K)
    experts.append(i0_blk[1] * nk + i1_rows[0])
    tv, te = [], []
    for _ in range(K):
        m, f, (e,) = _argmax_blocks(vals, flats, extras=(experts,))
        tv.append(m)
        te.append(e)
        vals = [jnp.where(t == f, -jnp.inf, b) for b, t in zip(vals, flats)]
    return tv, te


def _post_kernel(x_ref, y_ref, g1_ref, sh2_ref, sc2_ref, wout_ref, l1g_ref, l1b_ref, wpq_ref,
                 keys_ref, x1_ref, h2_ref, e_ref, g_ref, *, alpha):
    mix = _dot(y_ref[0], wout_ref[...])
    x1 = _ln(alpha * x_ref[0] + g1_ref[0] * mix, LN_EPS) * l1g_ref[...] + l1b_ref[...]
    x1_ref[0] = x1
    h2 = _ln(x1, LN_EPS) * (1.0 + sc2_ref[0]) + sh2_ref[0]
    chunks = h2.shape[1] // LANES
    for j in range(chunks):
        h2_ref[pl.ds(j, h2.shape[0], stride=chunks), :] = h2[:, j * LANES:(j + 1) * LANES]
    qall = _dot(h2, wpq_ref[...])
    nk = keys_ref.shape[1]
    K = PEER_TOPK
    e_rows, g_rows = [], []
    for h in range(PEER_HEADS):
        tops = []
        for c in range(2):
            lo = (h * 2 + c) * DH
            st = _dot_nt(keys_ref[c], qall[:, lo:lo + DH])
            tops.append(_topk_sorted(st, K))
        tv, te = _product_topk(tops[0], tops[1], nk)
        tv = jnp.concatenate(tv, axis=0)
        ex = jnp.exp(tv - tv[0:1, :])
        g_rows.append(ex / jnp.sum(ex, axis=0, keepdims=True))
        e_rows.append(jnp.concatenate(te, axis=0))
    e_all = jnp.concatenate(e_rows, axis=0) * ROW_SUB
    g_all = jnp.concatenate(g_rows, axis=0)
    e_ref[0] = pltpu.bitcast(pltpu.bitcast(e_all, F32).T, jnp.int32)
    g_ref[0] = g_all.T


def _post(x, ymix, g1, sh2, sc2, w_out, ln1_g, ln1_b, w_pq, keys, alpha, tm):
    B, T, D = x.shape
    npair = PEER_HEADS * PEER_TOPK
    full = lambda a: pl.BlockSpec(a.shape, lambda b, i: (0,) * a.ndim)
    tok = lambda w: pl.BlockSpec((1, tm, w), lambda b, i: (b, i, 0))
    bat = pl.BlockSpec((1, 1, D), lambda b, i: (b, 0, 0))
    return pl.pallas_call(
        functools.partial(_post_kernel, alpha=alpha),
        grid=(B, T // tm),
        in_specs=[tok(D), tok(D), bat, bat, bat, full(w_out), full(ln1_g), full(ln1_b),
                  full(w_pq), full(keys)],
        out_specs=[tok(D), pl.BlockSpec((tm * D // LANES, LANES), lambda b, i: (b * (T // tm) + i, 0)),
                   tok(npair), tok(npair)],
        out_shape=[jax.ShapeDtypeStruct((B, T, D), F32),
                   jax.ShapeDtypeStruct((B * T * D // LANES, LANES), F32),
                   jax.ShapeDtypeStruct((B, T, npair), jnp.int32),
                   jax.ShapeDtypeStruct((B, T, npair), F32)],
        compiler_params=pltpu.CompilerParams(dimension_semantics=("parallel", "parallel"),
                                             vmem_limit_bytes=VMEM_LIMIT),
        name="post",
    )(x, ymix, g1, sh2, sc2, w_out, ln1_g, ln1_b, w_pq, keys)


def _unpack_pair(words):
    lo = pltpu.bitcast(words << 16, F32)
    hi = pltpu.bitcast(words & jnp.int32(-65536), F32)
    return lo, hi


def _gelu_exact(x):
    return 0.5 * x * (1.0 + lax.erf(x * (1.0 / math.sqrt(2.0))))


def _index_copy(e_hbm, e_scrs, sems, step, s):
    words = e_scrs[s].shape[0]
    return pltpu.make_async_copy(e_hbm.at[pl.ds((step * 2 + s) * words, words)], e_scrs[s], sems.at[s])


def _with_prefetched_indices(e_hbm, e_scrs, sems, process):
    step = pl.program_id(0)

    @pl.when(step == 0)
    def _():
        _index_copy(e_hbm, e_scrs, sems, step, 0).start()

    _index_copy(e_hbm, e_scrs, sems, step, 1).start()
    _index_copy(e_hbm, e_scrs, sems, step, 0).wait()
    process(0, e_scrs[0])
    _index_copy(e_hbm, e_scrs, sems, step, 1).wait()

    @pl.when(step + 1 < pl.num_programs(0))
    def _():
        _index_copy(e_hbm, e_scrs, sems, step + 1, 0).start()

    process(1, e_scrs[1])


def _gather_pair_words(tbl_ref, e_idx, k, q, half, sub):
    ra = pl.multiple_of(e_idx.at[pl.ds(q * sub, sub)][k], ROW_SUB)
    rb = pl.multiple_of(e_idx.at[pl.ds((half + q) * sub, sub)][k], ROW_SUB)
    return jnp.concatenate([tbl_ref[pl.ds(ra, ROW_SUB), :], tbl_ref[pl.ds(rb, ROW_SUB), :]], axis=0)


def _peer_u_kernel(e_hbm, x_ref, g_ref, tbl_ref, w_ref, slot_a, slot_b, act_ref, e_scr0, e_scr1, sems,
                   *, tile):
    npair = PEER_HEADS * PEER_TOPK
    half = npair // 2
    sub = tile // 2
    lane = lax.broadcasted_iota(jnp.int32, (npair, LANES), 1)

    def process(s, e_idx):
        first = s * sub

        def gather(k, slot):
            t = first + k
            x8 = x_ref[pl.ds(pl.multiple_of(t * SUBLANES, SUBLANES), SUBLANES), :]
            xlo = jnp.concatenate([x8[0:ROW_SUB], x8[0:ROW_SUB]], axis=0)
            xhi = jnp.concatenate([x8[ROW_SUB:], x8[ROW_SUB:]], axis=0)
            for q in range(half):
                lo, hi = _unpack_pair(_gather_pair_words(tbl_ref, e_idx, k, q, half, sub))
                slot[pl.ds(q, SUBLANES, stride=SLOT_STRIDE), :] = lo * xlo + hi * xhi

        def reduce(k, slot):
            parts = []
            for g in range(2):
                part = slot[pl.ds(g * ROW_SUB * SLOT_STRIDE, half), :]
                for j in range(1, ROW_SUB):
                    part = part + slot[pl.ds((g * ROW_SUB + j) * SLOT_STRIDE, half), :]
                parts.append(part)
            part = jnp.concatenate(parts, axis=0)
            act = jnp.sum(part, axis=1, keepdims=True)
            act_ref[...] = jnp.where(lane == first + k, act, act_ref[...])

        gather(0, slot_a)

        slots = (slot_a, slot_b)

        def token_group(i, carry):
            k0 = PEER_UNROLL * i
            for j in range(PEER_UNROLL):
                nxt = k0 + j + 1 if j + 1 < PEER_UNROLL else jnp.minimum(k0 + PEER_UNROLL, sub - 1)
                gather(nxt, slots[(j + 1) % 2])
                reduce(k0 + j, slots[j % 2])
            return carry

        lax.fori_loop(0, sub // PEER_UNROLL, token_group, 0)

    act_ref[...] = jnp.zeros_like(act_ref)
    _with_prefetched_indices(e_hbm, (e_scr0, e_scr1), sems, process)
    act_ref[...] = g_ref[...].T * _gelu_exact(act_ref[...])
    top = lax.broadcasted_iota(jnp.int32, (SUBLANES, LANES), 0) < ROW_SUB
    wide = jnp.concatenate([jnp.where(top, act_ref[pl.ds(q, 1), :], act_ref[pl.ds(half + q, 1), :])
                            for q in range(half)], axis=0)
    w_ref[...] = wide.T


def _peer_u(eflat, x8, gw, table, tile):
    n, npair = gw.shape
    slot_rows = (SUBLANES - 1) * SLOT_STRIDE + npair // 2
    slot_rows = -(-slot_rows // SUBLANES) * SUBLANES
    tok = pl.BlockSpec((tile, npair), lambda i: (i, 0))
    return pl.pallas_call(
        functools.partial(_peer_u_kernel, tile=tile),
        grid=(n // tile,),
        in_specs=[pl.BlockSpec(memory_space=pl.ANY),
                  pl.BlockSpec((tile * SUBLANES, LANES), lambda i: (i, 0)),
                  tok,
                  pl.BlockSpec(memory_space=pltpu.VMEM)],
        out_specs=pl.BlockSpec((tile, npair // 2 * SUBLANES), lambda i: (i, 0)),
        out_shape=jax.ShapeDtypeStruct((n, npair // 2 * SUBLANES), F32),
        scratch_shapes=[pltpu.VMEM((slot_rows, LANES), F32),
                        pltpu.VMEM((slot_rows, LANES), F32),
                        pltpu.VMEM((npair, LANES), F32),
                        pltpu.SMEM((tile // 2 * npair,), jnp.int32),
                        pltpu.SMEM((tile // 2 * npair,), jnp.int32),
                        pltpu.SemaphoreType.DMA((2,))],
        compiler_params=pltpu.CompilerParams(dimension_semantics=("arbitrary",),
                                             vmem_limit_bytes=VMEM_LIMIT),
        name="peer_u",
    )(eflat, x8, gw, table)


def _peer_v_kernel(e_hbm, w_ref, tbl_ref, o_ref, wb_a, wb_b, e_scr0, e_scr1, sems, *, tile):
    npair = PEER_HEADS * PEER_TOPK
    half = npair // 2
    sub = tile // 2
    nacc = 4

    def process(s, e_idx):
        first = s * sub

        def spread(k, wb):
            row = w_ref[pl.ds(first + k, 1), :]
            wb[...] = jnp.broadcast_to(row, (LANES, row.shape[1])).T

        def accumulate(k, wb):
            acc_lo = [jnp.zeros((SUBLANES, LANES), F32) for _ in range(nacc)]
            acc_hi = [jnp.zeros((SUBLANES, LANES), F32) for _ in range(nacc)]
            for q in range(half):
                lo, hi = _unpack_pair(_gather_pair_words(tbl_ref, e_idx, k, q, half, sub))
                w = wb[pl.ds(q * SUBLANES, SUBLANES), :]
                acc_lo[q % nacc] = acc_lo[q % nacc] + w * lo
                acc_hi[q % nacc] = acc_hi[q % nacc] + w * hi
            lo = (acc_lo[0] + acc_lo[1]) + (acc_lo[2] + acc_lo[3])
            hi = (acc_hi[0] + acc_hi[1]) + (acc_hi[2] + acc_hi[3])
            lo = lo[0:ROW_SUB] + lo[ROW_SUB:]
            hi = hi[0:ROW_SUB] + hi[ROW_SUB:]
            rows = pl.ds(pl.multiple_of((first + k) * SUBLANES, SUBLANES), SUBLANES)
            o_ref[rows, :] = jnp.concatenate([lo, hi], axis=0)

        spread(0, wb_a)

        wbs = (wb_a, wb_b)

        def token_group(i, carry):
            k0 = PEER_UNROLL * i
            for j in range(PEER_UNROLL):
                nxt = k0 + j + 1 if j + 1 < PEER_UNROLL else jnp.minimum(k0 + PEER_UNROLL, sub - 1)
                spread(nxt, wbs[(j + 1) % 2])
                accumulate(k0 + j, wbs[j % 2])
            return carry

        lax.fori_loop(0, sub // PEER_UNROLL, token_group, 0)

    _with_prefetched_indices(e_hbm, (e_scr0, e_scr1), sems, process)


def _peer_v(eflat, w2, table, tile):
    n, wide = w2.shape
    npair = PEER_HEADS * PEER_TOPK
    return pl.pallas_call(
        functools.partial(_peer_v_kernel, tile=tile),
        grid=(n // tile,),
        in_specs=[pl.BlockSpec(memory_space=pl.ANY),
                  pl.BlockSpec((tile, wide), lambda i: (i, 0)),
                  pl.BlockSpec(memory_space=pltpu.VMEM)],
        out_specs=pl.BlockSpec((tile * SUBLANES, LANES), lambda i: (i, 0)),
        out_shape=jax.ShapeDtypeStruct((n * SUBLANES, LANES), F32),
        scratch_shapes=[pltpu.VMEM((wide, LANES), F32),
                        pltpu.VMEM((wide, LANES), F32),
                        pltpu.SMEM((tile // 2 * npair,), jnp.int32),
                        pltpu.SMEM((tile // 2 * npair,), jnp.int32),
                        pltpu.SemaphoreType.DMA((2,))],
        compiler_params=pltpu.CompilerParams(dimension_semantics=("arbitrary",),
                                             vmem_limit_bytes=VMEM_LIMIT),
        name="peer_v",
    )(eflat, w2, table)


def _pack_table(t):
    bits = lax.bitcast_convert_type(t.astype(BF16), jnp.uint16).astype(jnp.uint32)
    words = bits[:, :ROW_WORDS] | (bits[:, ROW_WORDS:] << 16)
    return lax.bitcast_convert_type(words, jnp.int32).reshape(t.shape[0] * ROW_SUB, LANES)


def _final_kernel(x_ref, p_ref, g2_ref, lg_ref, lb_ref, o_ref, *, alpha):
    tm, D = x_ref.shape[1], x_ref.shape[2]
    chunks = D // LANES
    peer = jnp.concatenate([p_ref[pl.ds(j, tm, stride=chunks), :] for j in range(chunks)], axis=1)
    o_ref[0] = _ln(alpha * x_ref[0] + g2_ref[0] * peer, LN_EPS) * lg_ref[...] + lb_ref[...]


def _final(x1, peer, g2, ln2_g, ln2_b, alpha, tm):
    B, T, D = x1.shape
    tok = pl.BlockSpec((1, tm, D), lambda b, i: (b, i, 0))
    vec = pl.BlockSpec((1, D), lambda b, i: (0, 0))
    return pl.pallas_call(
        functools.partial(_final_kernel, alpha=alpha),
        grid=(B, T // tm),
        in_specs=[tok, pl.BlockSpec((tm * D // LANES, LANES), lambda b, i: (b * (T // tm) + i, 0)),
                  pl.BlockSpec((1, 1, D), lambda b, i: (b, 0, 0)), vec, vec],
        out_specs=tok,
        out_shape=jax.ShapeDtypeStruct((B, T, D), F32),
        compiler_params=pltpu.CompilerParams(dimension_semantics=("parallel", "parallel")),
        name="final",
    )(x1, peer, g2, ln2_g, ln2_b)


def _retention_consts(chunk, t_valid):
    lg = jnp.log1p(-(2.0 ** (-5.0 - jnp.arange(N_HEADS, dtype=F32))))
    pos = jnp.arange(chunk, dtype=F32)
    diff = pos[:, None] - pos[None, :]
    dmat = jnp.where(diff >= 0, jnp.exp(lg[:, None, None] * jnp.maximum(diff, 0.0)), 0.0)
    inter = jnp.exp(lg[:, None] * (pos + 1.0))[..., None]
    wend = jnp.where(pos < t_valid, jnp.exp(lg[:, None] * (t_valid - 1.0 - pos)), 0.0)[..., None]
    decay = jnp.exp(lg * t_valid)[:, None, None]
    return dmat, inter, wend, decay


def _rotary_tables(T, pos0):
    half = DH // 2
    inv = ROPE_BASE ** (-jnp.arange(half, dtype=F32) / half)
    ang = (pos0 + jnp.arange(T, dtype=F32))[:, None] * inv[None, :]
    cos, sin = jnp.cos(ang), jnp.sin(ang)
    return jnp.concatenate([cos, cos], axis=-1), jnp.concatenate([-sin, sin], axis=-1)


def _layer(x, c, pos0, t_valid, C0, n0, m0, S0, p, alpha):
    B, T, D = x.shape
    W = N_HEADS * DH
    mod = _ada(c, p["w_ada"], p["b_ada"])
    sh1, sc1, g1, sh2, sc2, g2 = [m[:, None, :] for m in jnp.split(mod, 6, axis=-1)]

    w_in, b_in = p["w_in"], p["b_in"]
    g_off = 4 * W
    r_off = g_off + 2 * N_HEADS
    wm = w_in[:, :g_off].astype(BF16)
    wr = w_in[:, r_off:].astype(BF16)
    wg = jnp.pad(w_in[:, g_off:r_off], ((0, 0), (0, LANES - 2 * N_HEADS)))
    bg = jnp.pad(b_in[g_off:r_off], (0, LANES - 2 * N_HEADS))[None, :]
    wgt = w_in[:, g_off:r_off].T
    bgt = b_in[g_off:r_off][:, None]
    tm = min(T, 512)
    zm, zr, zg, zgt = _inproj(x, sh1, sc1, wm, b_in[None, :g_off], wr, b_in[None, r_off:],
                              wg, bg, wgt, bgt, tm)

    chunk = min(T, MIX_CHUNK)
    tv = min(t_valid, chunk) if T == chunk else chunk
    cos2, sin2 = _rotary_tables(T, pos0)
    dmat, inter, wend, decay = _retention_consts(chunk, tv)
    bc = lambda a: jnp.broadcast_to(a[..., None, :], a.shape[:-1] + (SUBLANES, a.shape[-1]))
    n0b = bc(n0.astype(F32))
    m0b = jnp.broadcast_to(m0.astype(F32)[..., None, None], m0.shape + (SUBLANES, DH))
    ymix, C1, n1, m1, S1 = _mixers(zm, zr, zg, zgt, cos2, sin2, dmat, inter, wend, decay,
                                   p["head_g"][None, :], C0.astype(F32), n0b, m0b, S0.astype(F32),
                                   chunk, tv)
    n1 = n1[:, :, 0, :]
    m1 = m1[:, :, 0, 0]

    x1, h2, eidx, gw = _post(x, ymix, g1, sh2, sc2, p["w_out"].astype(BF16), p["ln1_g"][None, :],
                             p["ln1_b"][None, :], p["w_pq"].astype(BF16), p["sub_keys"], alpha,
                             min(T, POST_TILE))
    npair = PEER_HEADS * PEER_TOPK
    chunks = D // LANES
    tile = PEER_TILE
    sub = tile // 2
    nv = B * t_valid
    n = -(-nv // tile) * tile

    def valid_rows(a, per_token):
        if t_valid == T and n == nv:
            return a
        a = a.reshape(B, T * per_token, a.shape[-1])[:, :t_valid * per_token]
        return jnp.pad(a.reshape(nv * per_token, a.shape[-1]), ((0, (n - nv) * per_token), (0, 0)))

    x8 = valid_rows(h2, chunks)
    e2 = valid_rows(eidx.reshape(B * T, npair), 1)
    gw2 = valid_rows(gw.reshape(B * T, npair), 1)
    eflat = e2.reshape(n // sub, sub, npair).transpose(0, 2, 1).reshape(n * npair)
    wts = _peer_u(eflat, x8, gw2, _pack_table(p["expert_u"]), tile)
    peer = _peer_v(eflat, wts, _pack_table(p["expert_v"]), tile)
    if not (t_valid == T and n == nv):
        peer = peer[:nv * chunks].reshape(B, t_valid * chunks, LANES)
        peer = jnp.pad(peer, ((0, 0), (0, (T - t_valid) * chunks), (0, 0))).reshape(B * T * chunks, LANES)
    y = _final(x1, peer, g2, p["ln2_g"][None, :], p["ln2_b"][None, :], alpha, min(T, 512))
    return y, (C1, n1, m1, S1)


def kernel(x_prompt, x_sample, c_prompt, c_sample, state_mlstm_C, state_mlstm_n, state_mlstm_m,
           state_ret_S, w_ada, b_ada, w_in, b_in, head_g, w_out, ln1_g, ln1_b, w_pq, sub_keys,
           expert_u, expert_v, ln2_g, ln2_b):
    depth = w_ada.shape[0]
    alpha = (2.0 * depth) ** 0.25
    params = dict(w_ada=w_ada, b_ada=b_ada, w_in=w_in, b_in=b_in, head_g=head_g, w_out=w_out,
                  ln1_g=ln1_g, ln1_b=ln1_b, w_pq=w_pq, sub_keys=sub_keys, expert_u=expert_u,
                  expert_v=expert_v, ln2_g=ln2_g, ln2_b=ln2_b)

    def run(x, c, pos0, C0, n0, m0, S0):
        B, T, D = x.shape
        t_valid = T
        if T % LANES:
            x = jnp.pad(x, ((0, 0), (0, LANES - T % LANES), (0, 0)))
        Cs, ns, ms, Ss = [], [], [], []
        for l in range(depth):
            pl_ = {k: v[l] for k, v in params.items()}
            x, (C1, n1, m1, S1) = _layer(x, c, pos0, t_valid, C0[l], n0[l], m0[l], S0[l], pl_, alpha)
            Cs.append(C1); ns.append(n1); ms.append(m1); Ss.append(S1)
        return x[:, :T], jnp.stack(Cs), jnp.stack(ns), jnp.stack(ms), jnp.stack(Ss)

    B = x_prompt.shape[0]
    H = state_mlstm_C.shape[2]
    zC = jnp.zeros((depth, B, H, DH, DH), F32)
    zn = jnp.zeros((depth, B, H, DH), F32)
    zm = jnp.zeros((depth, B, H), F32)
    zS = jnp.zeros((depth, B, H, DH, DH), F32)
    past_len = 1024
    y_p, C_p, n_p, m_p, S_p = run(x_prompt, c_prompt, 0, zC, zn, zm, zS)
    y_s, C_s, n_s, m_s, S_s = run(x_sample, c_sample, past_len, state_mlstm_C, state_mlstm_n,
                                  state_mlstm_m, state_ret_S)
    return (y_p, y_s, C_p, n_p, m_p, S_p, C_s, n_s, m_s, S_s)
```
